```python
import jax
import jax.numpy as jnp
from jax import lax
import numpy as np

D_MODEL = 1024
BATCH = 16
SEQ = 2048
DEPTH = 4

GRID_W = 64
CTX_LEN = 256
EPS = 1e-6
N_BRANCH = 3
N_MODS = 6

N_HEADS = 8
Q_RANK = 256
KV_RANK = 128
NOPE_DIM = 64
ROPE_DIM = 32
V_DIM = 64
MLA_W = N_HEADS * V_DIM
SM_SCALE = (NOPE_DIM + ROPE_DIM) ** -0.5
ROPE_THETA = 10000.0
Q_BLOCK = 128

LRU_W = D_MODEL
LRU_BLOCKS = 8
LRU_BS = LRU_W // LRU_BLOCKS
LRU_CONV = 4
LRU_C = 8.0

POOL_WINDOWS = (2, 4, 8, 16)
POOL_W = D_MODEL // 2
POOL_G = POOL_W // len(POOL_WINDOWS)

D_FF = 2816
FFN_CONV = 3

COL_KV = KV_RANK
COL_KR = COL_KV + ROPE_DIM
COL_UX = COL_KR + LRU_W
COL_Q = COL_UX + Q_RANK
COL_UY = COL_Q + LRU_W
COL_POOL = COL_UY + POOL_W
IN_COLS = COL_POOL + N_BRANCH * D_MODEL

kernel_name = 'hybrid_mla_rglru_pool_dit'


def rms_norm(x, g):
    xf = x.astype(jnp.float32)
    y = xf * lax.rsqrt(jnp.mean(xf * xf, axis=-1, keepdims=True) + EPS)
    return (y * g.astype(jnp.float32)).astype(x.dtype)


def adaln_params(cond, w, b):
    m = (jax.nn.silu(cond) @ w + b)[..., None, :]
    return jnp.split(m, N_MODS, axis=-1)


def modulate(h, shift, scale):
    return h * (1.0 + scale) + shift


def dw_conv(x, w, b, left, right):
    y = lax.conv_general_dilated(x, w[:, None, :], window_strides=(1,), padding=[(left, right)],
                                 dimension_numbers=('NWC', 'WIO', 'NWC'),
                                 feature_group_count=x.shape[-1])
    return y + b


def axial_rope_tables(row_pos, col_pos):
    half = ROPE_DIM // 2
    inv = ROPE_THETA ** (-jnp.arange(0, half, 2, dtype=jnp.float32) / half)
    ang_r = row_pos.astype(jnp.float32)[:, None] * inv
    ang_c = col_pos.astype(jnp.float32)[:, None] * inv
    return (jnp.cos(ang_r), jnp.sin(ang_r), jnp.cos(ang_c), jnp.sin(ang_c))


def rotate_half(x, cos, sin):
    x1, x2 = jnp.split(x, 2, axis=-1)
    return jnp.concatenate([x1 * cos - x2 * sin, x2 * cos + x1 * sin], axis=-1)


def apply_axial_rope(x, tabs):
    cr, sr, cc, sc = tabs
    x_row, x_col = jnp.split(x, 2, axis=-1)
    return jnp.concatenate([rotate_half(x_row, cr, sr), rotate_half(x_col, cc, sc)], axis=-1).astype(x.dtype)


def mla_q(cq, lp):
    q = (rms_norm(cq, lp['q_norm_g']) @ lp['w_uq']) * SM_SCALE
    q = q.reshape(*cq.shape[:-1], N_HEADS, NOPE_DIM + ROPE_DIM)
    return q[..., :NOPE_DIM], q[..., NOPE_DIM:]


def mla_kv(ckv, lp):
    kv = rms_norm(ckv, lp['kv_norm_g']) @ lp['w_ukv']
    kv = kv.reshape(*ckv.shape[:-1], N_HEADS, NOPE_DIM + V_DIM)
    return kv[..., :NOPE_DIM], kv[..., NOPE_DIM:]


def attend(qn, qr, kn, kr, v):
    s = jnp.einsum('bqhd,bkhd->bhqk', qn, kn) + jnp.einsum('bqhr,bkr->bhqk', qr, kr)
    p = jax.nn.softmax(s.astype(jnp.float32), axis=-1).astype(v.dtype)
    return jnp.einsum('bhqk,bkhd->bqhd', p, v)


def blockwise_attention(qn, qr, kn, kr, v):
    b, n, h, _ = qn.shape
    nb = n // Q_BLOCK

    def to_blocks(t):
        return jnp.moveaxis(t.reshape(b, nb, Q_BLOCK, *t.shape[2:]), 1, 0)

    out = lax.map(lambda qs: attend(qs[0], qs[1], kn, kr, v), (to_blocks(qn), to_blocks(qr)))
    return jnp.moveaxis(out, 0, 1).reshape(b, n, h * V_DIM)


def rglru_coeffs(u, lp, d):
    ub = u.reshape(*u.shape[:-1], LRU_BLOCKS, LRU_BS)
    r = jax.nn.sigmoid((jnp.einsum('blnc,ncd->blnd', ub, lp['lru_wa'][d]).reshape(u.shape)
                        + lp['lru_ba'][d]).astype(jnp.float32))
    i = jax.nn.sigmoid((jnp.einsum('blnc,ncd->blnd', ub, lp['lru_wi'][d]).reshape(u.shape)
                        + lp['lru_bi'][d]).astype(jnp.float32))
    log_a = -LRU_C * r * jax.nn.softplus(-lp['lru_lambda'][d].astype(jnp.float32))
    a = jnp.exp(log_a)
    bx = jnp.sqrt(-jnp.expm1(2.0 * log_a)) * i * u.astype(jnp.float32)
    return a, bx


def linear_scan(a, bx, h0, reverse, keep_states):
    def step(h, ab):
        h = ab[0] * h + ab[1]
        return h, (h if keep_states else None)

    h_last, hs = lax.scan(step, h0, (jnp.swapaxes(a, 0, 1), jnp.swapaxes(bx, 0, 1)), reverse=reverse)
    return h_last, (jnp.swapaxes(hs, 0, 1) if keep_states else None)


def centred_mean(u, window):
    n = u.shape[1]
    left = window // 2
    right = window - 1 - left
    cs = jnp.pad(jnp.cumsum(u, axis=1), ((0, 0), (1, 0), (0, 0)))
    t = jnp.arange(n)
    lo = jnp.clip(t - left, 0, n)
    hi = jnp.clip(t + right + 1, 0, n)
    cnt = (hi - lo).astype(jnp.float32)
    return (cs[:, hi] - cs[:, lo]) / cnt[None, :, None]


def pool_mix(u, lp):
    groups = jnp.split(u.astype(jnp.float32), len(POOL_WINDOWS), axis=-1)
    d = jnp.stack([centred_mean(g, w) - g for g, w in zip(groups, POOL_WINDOWS)], axis=-2).astype(u.dtype)
    y = jnp.einsum('blgc,gcd->blgd', d, lp['pool_w']).reshape(u.shape) + lp['pool_b']
    return y * lp['pool_scale']


def merge_branches(att, rec, pool, gate_logits, lp):
    g = jax.nn.sigmoid(gate_logits.reshape(*gate_logits.shape[:-1], N_BRANCH, D_MODEL))
    m = (g[..., 0, :] * (att @ lp['proj_mla']) + g[..., 1, :] * (rec @ lp['proj_lru'])
         + g[..., 2, :] * (pool @ lp['proj_pool']))
    return m @ lp['w_out']


def conv_ffn(h, lp):
    val, gate = jnp.split(h @ lp['ffn_up'], 2, axis=-1)
    gate = dw_conv(gate, lp['ffn_conv_w'], lp['ffn_conv_b'], FFN_CONV // 2, FFN_CONV - 1 - FFN_CONV // 2)
    return (jax.nn.silu(gate) * val) @ lp['ffn_down']


def hybrid_layer(x, xc, c, c_ctx, lp, tabs, last):
    bsz, n_lat, _ = x.shape
    n_ctx = xc.shape[1]
    conv_l, conv_r = LRU_CONV // 2, LRU_CONV - 1 - LRU_CONV // 2
    mx = adaln_params(c, lp['ada_w'], lp['ada_b'])
    mc = adaln_params(c_ctx, lp['ada_w'], lp['ada_b'])

    hx = modulate(rms_norm(x, lp['norm1_g']), mx[0], mx[1])
    hc = modulate(rms_norm(xc, lp['norm1_g']), mc[0], mc[1])
    px = hx @ lp['w_in']
    pc = hc @ (lp['w_in'][:, :COL_UX] if last else lp['w_in'])

    kn_c, v_c = mla_kv(pc[..., :COL_KV], lp)
    kr_c = pc[..., COL_KV:COL_KR]
    u_c = dw_conv(pc[..., COL_KR:COL_UX], lp['lru_conv_w'], lp['lru_conv_b'], conv_l, conv_r)
    h0 = jnp.zeros((bsz, LRU_W), jnp.float32)
    af_c, bf_c = rglru_coeffs(u_c, lp, 0)
    ab_c, bb_c = rglru_coeffs(u_c, lp, 1)
    hf_c_last, hf_c = linear_scan(af_c, bf_c, h0, False, not last)
    hb_c_last, hb_c = linear_scan(ab_c, bb_c, h0, True, not last)

    tabs_h = tuple(t[:, None, :] for t in tabs)
    kn_x, v_x = mla_kv(px[..., :COL_KV], lp)
    kr_x = apply_axial_rope(px[..., COL_KV:COL_KR], tabs)
    qn_x, qr_x = mla_q(px[..., COL_UX:COL_Q], lp)
    qr_x = apply_axial_rope(qr_x, tabs_h)
    att_x = blockwise_attention(qn_x, qr_x,
                                jnp.concatenate([kn_x, kn_c], axis=1),
                                jnp.concatenate([kr_x, kr_c], axis=1),
                                jnp.concatenate([v_x, v_c], axis=1))

    u_x = dw_conv(px[..., COL_KR:COL_UX], lp['lru_conv_w'], lp['lru_conv_b'], conv_l, conv_r)
    af_x, bf_x = rglru_coeffs(u_x, lp, 0)
    ab_x, bb_x = rglru_coeffs(u_x, lp, 1)
    _, hf_x = linear_scan(af_x, bf_x, hf_c_last, False, True)
    _, hb_x = linear_scan(ab_x, bb_x, hb_c_last, True, True)
    rec_x = ((hf_x + hb_x) * jax.nn.gelu(px[..., COL_Q:COL_UY].astype(jnp.float32))).astype(x.dtype)

    pool_x = pool_mix(px[..., COL_UY:COL_POOL], lp)

    x = x + mx[2] * merge_branches(att_x, rec_x, pool_x, px[..., COL_POOL:], lp)
    x = x + mx[5] * conv_ffn(modulate(rms_norm(x, lp['norm2_g']), mx[3], mx[4]), lp)
    if last:
        return x, None

    qn_c, qr_c = mla_q(pc[..., COL_UX:COL_Q], lp)
    att_c = attend(qn_c, qr_c, kn_c, kr_c, v_c).reshape(bsz, n_ctx, MLA_W)
    rec_c = ((hf_c + hb_c) * jax.nn.gelu(pc[..., COL_Q:COL_UY].astype(jnp.float32))).astype(xc.dtype)
    pool_c = pool_mix(pc[..., COL_UY:COL_POOL], lp)
    xc = xc + mc[2] * merge_branches(att_c, rec_c, pool_c, pc[..., COL_POOL:], lp)
    xc = xc + mc[5] * conv_ffn(modulate(rms_norm(xc, lp['norm2_g']), mc[3], mc[4]), lp)
    return x, xc


def setup_inputs(seed: int = 0) -> dict:
    key = jax.random.key(seed)
    ks = iter(jax.random.split(key, 48))
    f32 = jnp.float32
    L = DEPTH

    def nrm(shape, fan_in, s=1.0):
        return (s * fan_in ** -0.5) * jax.random.normal(next(ks), shape, f32)

    def gain(shape):
        return 1.0 + 0.02 * jax.random.normal(next(ks), shape, f32)

    def small(shape):
        return 0.02 * jax.random.normal(next(ks), shape, f32)

    x = jax.random.normal(next(ks), (BATCH, SEQ, D_MODEL), f32)
    c = jax.random.normal(next(ks), (BATCH, D_MODEL), f32)
    ctx = jax.random.normal(next(ks), (BATCH, CTX_LEN, D_MODEL), f32)
    c_ctx = jax.random.normal(next(ks), (D_MODEL,), f32)
    a_c = jax.random.uniform(next(ks), (L, 2, LRU_W), f32, 0.9, 0.999)
    s = a_c ** (1.0 / LRU_C)
    lru_lambda = jnp.log(s) - jnp.log1p(-s)
    return {
        'x': x,
        'c': c,
        'ctx': ctx,
        'c_ctx': c_ctx,
        'ada_w': nrm((L, D_MODEL, N_MODS * D_MODEL), D_MODEL, 0.5),
        'ada_b': small((L, N_MODS * D_MODEL)),
        'norm1_g': gain((L, D_MODEL)),
        'norm2_g': gain((L, D_MODEL)),
        'w_in': nrm((L, D_MODEL, IN_COLS), D_MODEL),
        'q_norm_g': gain((L, Q_RANK)),
        'w_uq': nrm((L, Q_RANK, N_HEADS * (NOPE_DIM + ROPE_DIM)), Q_RANK),
        'kv_norm_g': gain((L, KV_RANK)),
        'w_ukv': nrm((L, KV_RANK, N_HEADS * (NOPE_DIM + V_DIM)), KV_RANK),
        'lru_conv_w': nrm((L, LRU_CONV, LRU_W), LRU_CONV),
        'lru_conv_b': small((L, LRU_W)),
        'lru_wa': nrm((L, 2, LRU_BLOCKS, LRU_BS, LRU_BS), LRU_BS),
        'lru_ba': small((L, 2, LRU_W)),
        'lru_wi': nrm((L, 2, LRU_BLOCKS, LRU_BS, LRU_BS), LRU_BS),
        'lru_bi': small((L, 2, LRU_W)),
        'lru_lambda': lru_lambda,
        'pool_w': nrm((L, len(POOL_WINDOWS), POOL_G, POOL_G), POOL_G),
        'pool_b': small((L, POOL_W)),
        'pool_scale': gain((L, POOL_W)),
        'proj_mla': nrm((L, MLA_W, D_MODEL), MLA_W),
        'proj_lru': nrm((L, LRU_W, D_MODEL), LRU_W),
        'proj_pool': nrm((L, POOL_W, D_MODEL), POOL_W),
        'w_out': nrm((L, D_MODEL, D_MODEL), D_MODEL),
        'ffn_up': nrm((L, D_MODEL, 2 * D_FF), D_MODEL),
        'ffn_conv_w': nrm((L, FFN_CONV, D_FF), FFN_CONV),
        'ffn_conv_b': small((L, D_FF)),
        'ffn_down': nrm((L, D_FF, D_MODEL), D_FF),
        'final_norm_g': gain((D_MODEL,)),
    }


def reference(x, c, ctx, c_ctx, ada_w, ada_b, norm1_g, norm2_g, w_in, q_norm_g, w_uq, kv_norm_g, w_ukv,
              lru_conv_w, lru_conv_b, lru_wa, lru_ba, lru_wi, lru_bi, lru_lambda, pool_w, pool_b,
              pool_scale, proj_mla, proj_lru, proj_pool, w_out, ffn_up, ffn_conv_w, ffn_conv_b,
              ffn_down, final_norm_g):
    n_lat = x.shape[1]
    rows = n_lat // GRID_W
    row_pos = jnp.repeat(jnp.arange(rows), GRID_W)
    col_pos = jnp.tile(jnp.arange(GRID_W), rows)
    tabs = axial_rope_tables(row_pos, col_pos)
    xc = ctx
    for l in range(DEPTH):
        lp = dict(ada_w=ada_w[l], ada_b=ada_b[l], norm1_g=norm1_g[l], norm2_g=norm2_g[l], w_in=w_in[l],
                  q_norm_g=q_norm_g[l], w_uq=w_uq[l], kv_norm_g=kv_norm_g[l], w_ukv=w_ukv[l],
                  lru_conv_w=lru_conv_w[l], lru_conv_b=lru_conv_b[l], lru_wa=lru_wa[l], lru_ba=lru_ba[l],
                  lru_wi=lru_wi[l], lru_bi=lru_bi[l], lru_lambda=lru_lambda[l], pool_w=pool_w[l],
                  pool_b=pool_b[l], pool_scale=pool_scale[l], proj_mla=proj_mla[l], proj_lru=proj_lru[l],
                  proj_pool=proj_pool[l], w_out=w_out[l], ffn_up=ffn_up[l], ffn_conv_w=ffn_conv_w[l],
                  ffn_conv_b=ffn_conv_b[l], ffn_down=ffn_down[l])
        x, xc = hybrid_layer(x, xc, c, c_ctx, lp, tabs, l == DEPTH - 1)
    return rms_norm(x, final_norm_g)
```

```python
import functools

import jax
import jax.numpy as jnp
from jax import lax
from jax.experimental import pallas as pl
from jax.experimental.pallas import tpu as pltpu

F32 = jnp.float32
BF16 = jnp.bfloat16

D_MODEL = 1024
DEPTH = 4
GRID_W = 64
EPS = 1e-6
N_MODS = 6

N_HEADS = 8
Q_RANK = 256
KV_RANK = 128
NOPE_DIM = 64
ROPE_DIM = 32
V_DIM = 64
MLA_W = N_HEADS * V_DIM
SM_SCALE = (NOPE_DIM + ROPE_DIM) ** -0.5
ROPE_THETA = 10000.0

LRU_W = D_MODEL
LRU_BLOCKS = 8
LRU_BS = LRU_W // LRU_BLOCKS
LRU_CONV = 4
LRU_C = 8.0

POOL_WINDOWS = (2, 4, 8, 16)
POOL_W = D_MODEL // 2
POOL_G = POOL_W // len(POOL_WINDOWS)

D_FF = 2816
FFN_CONV = 3

COL_KV = KV_RANK
COL_KR = COL_KV + ROPE_DIM
COL_UX = COL_KR + LRU_W
COL_Q = COL_UX + Q_RANK
COL_UY = COL_Q + LRU_W
COL_POOL = COL_UY + POOL_W

LANE = 128
HEAD_PAD = LANE
PA_CKV, PA_KR, PA_KRP, PA_CQ = 0, 128, 256, 384
PA_UX = PA_CQ + Q_RANK
PA_UY = PA_UX + LRU_W
PA_POOL = PA_UY + LRU_W
PA_COLS = PA_POOL + POOL_W

ROWS_IN = 256
ROWS_MERGE = 256
ROWS_FFN = 256
Q_TILE = 256
T_CHUNK = 256
LRU_LANES = 128
POOL_HALO = max(POOL_WINDOWS) // 2
VMEM_LIMIT = 56 * 1024 * 1024


def _cparams(sem):
    return pltpu.CompilerParams(dimension_semantics=sem, vmem_limit_bytes=VMEM_LIMIT)


def _rms(x, g):
    return x * lax.rsqrt(jnp.mean(x * x, axis=-1, keepdims=True) + EPS) * g


def _modulate(y, shift, scale, nb):
    r, d = y.shape
    y3 = y.reshape(r // nb, nb, d)
    return (y3 * (1.0 + scale)[None] + shift[None]).reshape(r, d)


def _gate_rows(y, gate, nb):
    r, d = y.shape
    return (y.reshape(r // nb, nb, d) * gate[None]).reshape(r, d)


def _dot(a, b):
    return jnp.dot(a, b, preferred_element_type=F32)


def _ada_kernel(cond_ref, w_ref, b_ref, o_ref):
    c = cond_ref[...]
    s = (c * jax.nn.sigmoid(c)).astype(BF16)
    o_ref[...] = _dot(s, w_ref[...].astype(BF16)) + b_ref[...]


def _ada_call(cond, ada_w, ada_b):
    depth = ada_w.shape[0]
    rows = cond.shape[0]
    n_out = ada_w.shape[2]
    tn = n_out // 4
    return pl.pallas_call(
        _ada_kernel,
        out_shape=jax.ShapeDtypeStruct((depth, rows, n_out), F32),
        grid=(depth, n_out // tn),
        in_specs=[
            pl.BlockSpec((rows, D_MODEL), lambda l, j: (0, 0)),
            pl.BlockSpec((None, D_MODEL, tn), lambda l, j: (l, 0, j)),
            pl.BlockSpec((None, 1, tn), lambda l, j: (l, 0, j)),
        ],
        out_specs=pl.BlockSpec((None, rows, tn), lambda l, j: (l, 0, j)),
        compiler_params=_cparams(("parallel", "parallel")),
        name="ada",
    )(cond, ada_w, ada_b.reshape(depth, 1, n_out))


def _inproj_kernel(x_ref, mod_ref, g1_ref, wa_ref, kvg_ref, wkv_ref, qg_ref, wq_ref,
                   cos_ref, sin_ref, q_ref, k_ref, v_ref, ux_ref, guy_ref, pin_ref, *, nb):
    x = x_ref[...]
    shift = mod_ref[:, 0:D_MODEL]
    scale = mod_ref[:, D_MODEL:2 * D_MODEL]
    h = _modulate(_rms(x, g1_ref[...]), shift, scale, nb).astype(BF16)
    p = _dot(h, wa_ref[...])
    cos = cos_ref[...]
    sin = sin_ref[...]

    ux_ref[...] = p[:, PA_UX:PA_UX + LRU_W]
    guy_ref[...] = jax.nn.gelu(p[:, PA_UY:PA_UY + LRU_W]).astype(guy_ref.dtype)
    pin_ref[...] = p[:, PA_POOL:PA_POOL + POOL_W]

    ckvn = _rms(p[:, PA_CKV:PA_CKV + KV_RANK], kvg_ref[...]).astype(BF16)
    kv = _dot(ckvn, wkv_ref[...])
    kr = p[:, PA_KR:PA_KR + LANE] * cos + p[:, PA_KRP:PA_KRP + LANE] * sin
    k = kv[:, :N_HEADS * HEAD_PAD] + jnp.concatenate([kr] * N_HEADS, axis=1)
    k_ref[...] = k.astype(k_ref.dtype)
    v_ref[...] = kv[:, N_HEADS * HEAD_PAD:].astype(v_ref.dtype)

    cqn = _rms(p[:, PA_CQ:PA_CQ + Q_RANK], qg_ref[...]).astype(BF16)
    qq = _dot(cqn, wq_ref[...])
    hw = N_HEADS * HEAD_PAD
    cos_h = jnp.concatenate([cos] * N_HEADS, axis=1)
    sin_h = jnp.concatenate([sin] * N_HEADS, axis=1)
    q = (qq[:, :hw] + qq[:, hw:2 * hw] * cos_h + qq[:, 2 * hw:] * sin_h) * SM_SCALE
    q_ref[...] = q.astype(q_ref.dtype)


def _inproj_call(l, xt, mods, g1, w_a, kv_g, w_kv2, q_g, wq2, cos_t, sin_t, *, nb, n_ctx):
    rows = xt.shape[0]
    r = ROWS_IN
    ctx_tiles = (n_ctx * nb) // r
    seg = lambda i: jnp.where(i >= ctx_tiles, 1, 0)
    row_spec = lambda w: pl.BlockSpec((r, w), lambda i: (i, 0))
    hw = N_HEADS * HEAD_PAD
    return pl.pallas_call(
        functools.partial(_inproj_kernel, nb=nb),
        out_shape=(
            jax.ShapeDtypeStruct((rows, hw), BF16),
            jax.ShapeDtypeStruct((rows, hw), BF16),
            jax.ShapeDtypeStruct((rows, MLA_W), BF16),
            jax.ShapeDtypeStruct((rows, LRU_W), F32),
            jax.ShapeDtypeStruct((rows, LRU_W), BF16),
            jax.ShapeDtypeStruct((rows, POOL_W), F32),
        ),
        grid=(rows // r,),
        in_specs=[
            row_spec(D_MODEL),
            pl.BlockSpec((None, None, nb, N_MODS * D_MODEL), lambda i: (l, seg(i), 0, 0)),
            pl.BlockSpec((None, 1, D_MODEL), lambda i: (l, 0, 0)),
            pl.BlockSpec((None, D_MODEL, PA_COLS), lambda i: (l, 0, 0)),
            pl.BlockSpec((None, 1, KV_RANK), lambda i: (l, 0, 0)),
            pl.BlockSpec((None, KV_RANK, hw + MLA_W), lambda i: (l, 0, 0)),
            pl.BlockSpec((None, 1, Q_RANK), lambda i: (l, 0, 0)),
            pl.BlockSpec((None, Q_RANK, 3 * hw), lambda i: (l, 0, 0)),
            row_spec(LANE),
            row_spec(LANE),
        ],
        out_specs=(row_spec(hw), row_spec(hw), row_spec(MLA_W), row_spec(LRU_W),
                   row_spec(LRU_W), row_spec(POOL_W)),
        compiler_params=_cparams(("parallel",)),
        name="inproj",
    )(xt, mods, g1, w_a, kv_g, w_kv2, q_g, wq2, cos_t, sin_t)


def _attn_kernel(q_ref, k_ref, v_ref, o_ref, *, n_ctx, n_all):
    i = pl.program_id(1)
    tq = q_ref.shape[0]
    lane = lax.broadcasted_iota(jnp.int32, (tq, LANE), 1)

    def run(nk):
        for pair in range(N_HEADS // 2):
            vp = v_ref[0:nk, pair * LANE:(pair + 1) * LANE]
            outs = []
            for hh in range(2):
                h = 2 * pair + hh
                qh = q_ref[:, h * HEAD_PAD:(h + 1) * HEAD_PAD]
                kh = k_ref[0:nk, h * HEAD_PAD:(h + 1) * HEAD_PAD]
                s = lax.dot_general(qh, kh, (((1,), (1,)), ((), ())),
                                    preferred_element_type=F32)
                m = jnp.max(s, axis=-1, keepdims=True)
                e = jnp.exp(s - m)
                den = jnp.sum(e, axis=-1, keepdims=True)
                outs.append(_dot(e.astype(BF16), vp) * (1.0 / den))
            o_ref[:, pair * LANE:(pair + 1) * LANE] = jnp.where(
                lane < V_DIM, outs[0], outs[1]).astype(o_ref.dtype)

    ctx_tiles = n_ctx // tq

    @pl.when(i < ctx_tiles)
    def _():
        run(n_ctx)

    @pl.when(i >= ctx_tiles)
    def _():
        run(n_all)


def _attn_call(q, k, v, *, nb, n_ctx):
    rows = q.shape[0]
    t_all = rows // nb
    hw = N_HEADS * HEAD_PAD
    out = pl.pallas_call(
        functools.partial(_attn_kernel, n_ctx=n_ctx, n_all=t_all),
        out_shape=jax.ShapeDtypeStruct((t_all, nb * MLA_W), BF16),
        grid=(nb, t_all // Q_TILE),
        in_specs=[
            pl.BlockSpec((Q_TILE, hw), lambda b, i: (i, b)),
            pl.BlockSpec((t_all, hw), lambda b, i: (0, b)),
            pl.BlockSpec((t_all, MLA_W), lambda b, i: (0, b)),
        ],
        out_specs=pl.BlockSpec((Q_TILE, MLA_W), lambda b, i: (i, b)),
        compiler_params=_cparams(("parallel", "arbitrary")),
        name="attn",
    )(q.reshape(t_all, nb * hw), k.reshape(t_all, nb * hw), v.reshape(t_all, nb * MLA_W))
    return out.reshape(rows, MLA_W)


def _chunk_of(d, j, n_chunks, ctx_chunks):
    bwd = jnp.where(j < ctx_chunks, ctx_chunks - 1 - j, n_chunks - 1 - (j - ctx_chunks))
    return jnp.where(d == 0, j, bwd)


def _lru_kernel(ux_ref, uxl_ref, uxr_ref, cw_ref, cb_ref, wa_ref, wi_ref, ba_ref, bi_ref,
                lam_ref, o_ref, a_s, b_s, h_s, *, nb, n_chunks, ctx_chunks):
    d = pl.program_id(1)
    j = pl.program_id(2)
    c = _chunk_of(d, j, n_chunks, ctx_chunks)
    rc = ux_ref.shape[0]
    tc = rc // nb

    seg_start = jnp.logical_or(c == 0, c == ctx_chunks)
    seg_end = jnp.logical_or(c == ctx_chunks - 1, c == n_chunks - 1)
    left = uxl_ref[...] * jnp.where(seg_start, 0.0, 1.0)
    right = uxr_ref[...] * jnp.where(seg_end, 0.0, 1.0)
    ext = jnp.concatenate([left, ux_ref[...], right], axis=0)
    u = (cw_ref[0:1, :] * ext[0:rc] + cw_ref[1:2, :] * ext[nb:nb + rc]
         + cw_ref[2:3, :] * ext[2 * nb:2 * nb + rc] + cw_ref[3:4, :] * ext[3 * nb:3 * nb + rc]
         + cb_ref[...])

    for blk in range(u.shape[1] // LRU_BS):
        sl = slice(blk * LRU_BS, (blk + 1) * LRU_BS)
        ub = u[:, sl]
        ubb = ub.astype(BF16)
        r = jax.nn.sigmoid(_dot(ubb, wa_ref[blk]) + ba_ref[:, sl])
        gi = jax.nn.sigmoid(_dot(ubb, wi_ref[blk]) + bi_ref[:, sl])
        z = -lam_ref[:, sl]
        softplus = jnp.maximum(z, 0.0) + jnp.log(1.0 + jnp.exp(-jnp.abs(z)))
        a = jnp.exp(-LRU_C * r * softplus)
        a_s[:, sl] = a
        b_s[:, sl] = jnp.sqrt(1.0 - a * a) * gi * ub

    @pl.when(j == 0)
    def _():
        h_s[...] = jnp.zeros_like(h_s)

    def step(s, h):
        t = jnp.where(d == 0, s, tc - 1 - s)
        r0 = pl.multiple_of(t * nb, nb)
        h = a_s[pl.ds(r0, nb), :] * h + b_s[pl.ds(r0, nb), :]
        o_ref[pl.ds(r0, nb), :] = h
        return h

    h_s[...] = lax.fori_loop(0, tc, step, h_s[...], unroll=8)


def _lru_call(l, ux, conv_w, conv_b, wa, wi, ba, bi, lam, *, nb, n_ctx):
    rows = ux.shape[0]
    t_all = rows // nb
    n_chunks = t_all // T_CHUNK
    ctx_chunks = n_ctx // T_CHUNK
    rc = T_CHUNK * nb
    lw = LRU_LANES
    nblk = lw // LRU_BS
    chunk = lambda d, j: _chunk_of(d, j, n_chunks, ctx_chunks)
    vec_spec = pl.BlockSpec((None, None, 1, lw), lambda n, d, j: (l, d, 0, n))
    return pl.pallas_call(
        functools.partial(_lru_kernel, nb=nb, n_chunks=n_chunks, ctx_chunks=ctx_chunks),
        out_shape=jax.ShapeDtypeStruct((2, rows, LRU_W), F32),
        grid=(LRU_W // lw, 2, n_chunks),
        in_specs=[
            pl.BlockSpec((rc, lw), lambda n, d, j: (chunk(d, j), n)),
            pl.BlockSpec((2 * nb, lw),
                         lambda n, d, j: (jnp.maximum(chunk(d, j) * (T_CHUNK // 2) - 1, 0), n)),
            pl.BlockSpec((nb, lw),
                         lambda n, d, j: (jnp.minimum((chunk(d, j) + 1) * T_CHUNK, t_all - 1), n)),
            pl.BlockSpec((None, LRU_CONV, lw), lambda n, d, j: (l, 0, n)),
            pl.BlockSpec((None, 1, lw), lambda n, d, j: (l, 0, n)),
            pl.BlockSpec((None, None, nblk, LRU_BS, LRU_BS), lambda n, d, j: (l, d, n, 0, 0)),
            pl.BlockSpec((None, None, nblk, LRU_BS, LRU_BS), lambda n, d, j: (l, d, n, 0, 0)),
            vec_spec, vec_spec, vec_spec,
        ],
        out_specs=pl.BlockSpec((None, rc, lw), lambda n, d, j: (d, chunk(d, j), n)),
        scratch_shapes=[pltpu.VMEM((rc, lw), F32), pltpu.VMEM((rc, lw), F32),
                        pltpu.VMEM((nb, lw), F32)],
        compiler_params=_cparams(("parallel", "arbitrary", "arbitrary")),
        name="lru",
    )(ux, ux, ux, conv_w, conv_b, wa, wi, ba, bi, lam)


def _pool_diff(ext, t0, nb, rows, n_ctx, n_lat):
    halo = POOL_HALO * nb

    def shifted_sum(arr, start, k):
        n = arr.shape[0]
        return arr[2 * k:n] + arr[0:n - 2 * k], start + k

    s2 = ext[nb:] + ext[:-nb]
    st = nb
    sums = {2: (s2, st)}
    cur, cst = s2, st
    step = nb
    for w in POOL_WINDOWS[1:]:
        cur, cst = shifted_sum(cur, cst, step)
        sums[w] = (cur, cst)
        step *= 2

    t = t0 + lax.broadcasted_iota(jnp.int32, (rows // nb, nb, POOL_G), 0).reshape(rows, POOL_G)
    in_ctx = t < n_ctx
    t_loc = jnp.where(in_ctx, t, t - n_ctx)
    n_seg = jnp.where(in_ctx, n_ctx, n_lat)
    outs = []
    for g, w in enumerate(POOL_WINDOWS):
        arr, start = sums[w]
        lanes = slice(g * POOL_G, (g + 1) * POOL_G)
        tot = arr[halo - start:halo - start + rows, lanes]
        left = w // 2
        right = w - 1 - left
        cnt = (jnp.minimum(t_loc + right + 1, n_seg) - jnp.maximum(t_loc - left, 0)).astype(F32)
        outs.append(tot / cnt - ext[halo:halo + rows, lanes])
    return outs


def _merge_kernel(x_ref, mod_ref, g1_ref, att_ref, hs_ref, guy_ref, pin_ref, pl_ref, pr_ref,
                  wg_ref, pm_ref, plru_ref, pp_ref, pw_ref, pb_ref, ps_ref, wo_ref, o_ref,
                  *, nb, n_ctx, n_lat):
    i = pl.program_id(0)
    rows = x_ref.shape[0]
    tt = rows // nb
    t0 = i * tt
    n_all = n_ctx + n_lat

    x = x_ref[...]
    shift = mod_ref[:, 0:D_MODEL]
    scale = mod_ref[:, D_MODEL:2 * D_MODEL]
    gate1 = mod_ref[:, 2 * D_MODEL:3 * D_MODEL]
    h = _modulate(_rms(x, g1_ref[...]), shift, scale, nb).astype(BF16)

    seg_start = jnp.logical_or(t0 == 0, t0 == n_ctx)
    seg_end = jnp.logical_or(t0 + tt == n_ctx, t0 + tt == n_all)
    ext = jnp.concatenate([pl_ref[...] * jnp.where(seg_start, 0.0, 1.0), pin_ref[...],
                           pr_ref[...] * jnp.where(seg_end, 0.0, 1.0)], axis=0)
    diffs = _pool_diff(ext, t0, nb, rows, n_ctx, n_lat)
    pooled = [_dot(dg.astype(BF16), pw_ref[g]) for g, dg in enumerate(diffs)]
    pool = (jnp.concatenate(pooled, axis=1) + pb_ref[...]) * ps_ref[...]

    rec = ((hs_ref[0] + hs_ref[1]) * guy_ref[...].astype(F32)).astype(BF16)

    def gate(k):
        return jax.nn.sigmoid(_dot(h, wg_ref[:, k * D_MODEL:(k + 1) * D_MODEL]))

    m = gate(0) * _dot(att_ref[...], pm_ref[...])
    m = m + gate(1) * _dot(rec, plru_ref[...])
    m = m + gate(2) * _dot(pool.astype(BF16), pp_ref[...])
    y = _dot(m.astype(BF16), wo_ref[...])
    o_ref[...] = x + _gate_rows(y, gate1, nb)


def _merge_call(l, xt, mods, g1, att, hs, guy, pin, w_g, p_mla, p_lru, p_pool, pool_w, pool_b,
                pool_scale, w_out, *, nb, n_ctx, n_lat):
    rows = xt.shape[0]
    r = ROWS_MERGE
    hb = POOL_HALO * nb
    n_hblk = rows // hb
    ctx_tiles = (n_ctx * nb) // r
    seg = lambda i: jnp.where(i >= ctx_tiles, 1, 0)
    row_spec = lambda w: pl.BlockSpec((r, w), lambda i: (i, 0))
    wspec = lambda a, b: pl.BlockSpec((None, a, b), lambda i: (l, 0, 0))
    return pl.pallas_call(
        functools.partial(_merge_kernel, nb=nb, n_ctx=n_ctx, n_lat=n_lat),
        out_shape=jax.ShapeDtypeStruct((rows, D_MODEL), F32),
        grid=(rows // r,),
        in_specs=[
            row_spec(D_MODEL),
            pl.BlockSpec((None, None, nb, N_MODS * D_MODEL), lambda i: (l, seg(i), 0, 0)),
            wspec(1, D_MODEL),
            row_spec(MLA_W),
            pl.BlockSpec((2, r, LRU_W), lambda i: (0, i, 0)),
            row_spec(LRU_W),
            row_spec(POOL_W),
            pl.BlockSpec((hb, POOL_W), lambda i: (jnp.maximum(i * (r // hb) - 1, 0), 0)),
            pl.BlockSpec((hb, POOL_W), lambda i: (jnp.minimum((i + 1) * (r // hb), n_hblk - 1), 0)),
            wspec(D_MODEL, 3 * D_MODEL),
            wspec(MLA_W, D_MODEL),
            wspec(LRU_W, D_MODEL),
            wspec(POOL_W, D_MODEL),
            pl.BlockSpec((None, len(POOL_WINDOWS), POOL_G, POOL_G), lambda i: (l, 0, 0, 0)),
            wspec(1, POOL_W),
            wspec(1, POOL_W),
            wspec(D_MODEL, D_MODEL),
        ],
        out_specs=row_spec(D_MODEL),
        compiler_params=_cparams(("parallel",)),
        name="merge",
    )(xt, mods, g1, att, hs, guy, pin, pin, pin, w_g, p_mla, p_lru, p_pool, pool_w, pool_b,
      pool_scale, w_out)


def _ffn_kernel(x_ref, xl_ref, xr_ref, mod_ref, g2_ref, wv_ref, wgt_ref, cw_ref, cb_ref, wd_ref,
                o_ref, *, nb, n_ctx, n_lat):
    i = pl.program_id(0)
    rows = x_ref.shape[0]
    tt = rows // nb
    t0 = i * tt
    n_all = n_ctx + n_lat

    x = x_ref[...]
    shift = mod_ref[:, 3 * D_MODEL:4 * D_MODEL]
    scale = mod_ref[:, 4 * D_MODEL:5 * D_MODEL]
    gate2 = mod_ref[:, 5 * D_MODEL:6 * D_MODEL]
    g2 = g2_ref[...]

    xe = jnp.concatenate([xl_ref[...], x, xr_ref[...]], axis=0)
    he = _modulate(_rms(xe, g2), shift, scale, nb).astype(BF16)
    gate = _dot(he, wgt_ref[...])
    seg_start = jnp.logical_or(t0 == 0, t0 == n_ctx)
    seg_end = jnp.logical_or(t0 + tt == n_ctx, t0 + tt == n_all)
    g_prev = jnp.concatenate([gate[0:nb] * jnp.where(seg_start, 0.0, 1.0), gate[nb:rows]], axis=0)
    g_next = jnp.concatenate([gate[2 * nb:rows + nb],
                              gate[rows + nb:rows + 2 * nb] * jnp.where(seg_end, 0.0, 1.0)], axis=0)
    conv = (cw_ref[0:1, :] * g_prev + cw_ref[1:2, :] * gate[nb:nb + rows]
            + cw_ref[2:3, :] * g_next + cb_ref[...])
    val = _dot(he[nb:nb + rows], wv_ref[...])
    act = (conv * jax.nn.sigmoid(conv) * val).astype(BF16)
    y = _dot(act, wd_ref[...])
    o_ref[...] = x + _gate_rows(y, gate2, nb)


def _ffn_call(l, xt, mods, g2, w_val, w_gate, conv_w, conv_b, w_down, *, nb, n_ctx, n_lat):
    rows = xt.shape[0]
    t_all = rows // nb
    r = ROWS_FFN
    tt = r // nb
    ctx_tiles = (n_ctx * nb) // r
    seg = lambda i: jnp.where(i >= ctx_tiles, 1, 0)
    wspec = lambda a, b: pl.BlockSpec((None, a, b), lambda i: (l, 0, 0))
    return pl.pallas_call(
        functools.partial(_ffn_kernel, nb=nb, n_ctx=n_ctx, n_lat=n_lat),
        out_shape=jax.ShapeDtypeStruct((rows, D_MODEL), F32),
        grid=(rows // r,),
        in_specs=[
            pl.BlockSpec((r, D_MODEL), lambda i: (i, 0)),
            pl.BlockSpec((nb, D_MODEL), lambda i: (jnp.maximum(i * tt - 1, 0), 0)),
            pl.BlockSpec((nb, D_MODEL), lambda i: (jnp.minimum((i + 1) * tt, t_all - 1), 0)),
            pl.BlockSpec((None, None, nb, N_MODS * D_MODEL), lambda i: (l, seg(i), 0, 0)),
            wspec(1, D_MODEL),
            wspec(D_MODEL, D_FF),
            wspec(D_MODEL, D_FF),
            wspec(FFN_CONV, D_FF),
            wspec(1, D_FF),
            wspec(D_FF, D_MODEL),
        ],
        out_specs=pl.BlockSpec((r, D_MODEL), lambda i: (i, 0)),
        compiler_params=_cparams(("parallel",)),
        name="ffn",
    )(xt, xt, xt, mods, g2, w_val, w_gate, conv_w, conv_b, w_down)


def _final_kernel(x_ref, g_ref, o_ref):
    o_ref[...] = _rms(x_ref[...], g_ref[...])


def _final_call(xt, g, *, nb, n_ctx, n_lat):
    t_all = n_ctx + n_lat
    tq = Q_TILE
    off = n_ctx // tq
    return pl.pallas_call(
        _final_kernel,
        out_shape=jax.ShapeDtypeStruct((nb, n_lat, D_MODEL), F32),
        grid=(nb, n_lat // tq),
        in_specs=[
            pl.BlockSpec((tq, D_MODEL), lambda b, i: (i + off, b)),
            pl.BlockSpec((1, D_MODEL), lambda b, i: (0, 0)),
        ],
        out_specs=pl.BlockSpec((None, tq, D_MODEL), lambda b, i: (b, i, 0)),
        compiler_params=_cparams(("parallel", "parallel")),
        name="final_norm",
    )(xt.reshape(t_all, nb * D_MODEL), g.reshape(1, D_MODEL))


def _rope_tables(n_ctx, n_lat, nb):
    half = ROPE_DIM // 2
    quarter = half // 2
    inv = ROPE_THETA ** (-jnp.arange(0, half, 2, dtype=F32) / half)
    t = jnp.arange(n_lat)
    ang_r = (t // GRID_W).astype(F32)[:, None] * inv
    ang_c = (t % GRID_W).astype(F32)[:, None] * inv
    cr, sr, cc, sc = jnp.cos(ang_r), jnp.sin(ang_r), jnp.cos(ang_c), jnp.sin(ang_c)
    cos32 = jnp.concatenate([cr, cr, cc, cc], axis=1)
    sin32 = jnp.concatenate([-sr, sr, -sc, sc], axis=1)
    assert cos32.shape[1] == 4 * quarter == ROPE_DIM
    cos32 = jnp.concatenate([jnp.ones((n_ctx, ROPE_DIM), F32), cos32], axis=0)
    sin32 = jnp.concatenate([jnp.zeros((n_ctx, ROPE_DIM), F32), sin32], axis=0)
    pad = lambda a: jnp.pad(a, ((0, 0), (NOPE_DIM, HEAD_PAD - NOPE_DIM - ROPE_DIM)))
    expand = lambda a: jnp.repeat(pad(a), nb, axis=0)
    return expand(cos32), expand(sin32)


def _rope_partner(w):
    q = ROPE_DIM // 4
    return jnp.concatenate([w[..., q:2 * q], w[..., 0:q], w[..., 3 * q:4 * q], w[..., 2 * q:3 * q]],
                           axis=-1)


def _pack_weights(w_in, w_uq, w_ukv):
    depth = w_in.shape[0]
    zeros = lambda n: jnp.zeros((depth, D_MODEL, n), w_in.dtype)
    w_kr = w_in[:, :, COL_KV:COL_KR]
    tail = HEAD_PAD - NOPE_DIM - ROPE_DIM
    kr_grp = lambda w: jnp.concatenate([zeros(NOPE_DIM), w, zeros(tail)], axis=-1)
    w_a = jnp.concatenate([
        w_in[:, :, :COL_KV], kr_grp(w_kr), kr_grp(_rope_partner(w_kr)),
        w_in[:, :, COL_UX:COL_Q], w_in[:, :, COL_KR:COL_UX], w_in[:, :, COL_Q:COL_UY],
        w_in[:, :, COL_UY:COL_POOL]], axis=-1).astype(BF16)
    w_g = w_in[:, :, COL_POOL:].astype(BF16)

    uq = w_uq.reshape(depth, Q_RANK, N_HEADS, NOPE_DIM + ROPE_DIM)
    qn, qr = uq[..., :NOPE_DIM], uq[..., NOPE_DIM:]
    padq = lambda a, lo: jnp.pad(a, ((0, 0), (0, 0), (0, 0), (lo, HEAD_PAD - lo - a.shape[-1])))
    flat = lambda a: a.reshape(depth, a.shape[1], N_HEADS * HEAD_PAD)
    wq2 = jnp.concatenate([flat(padq(qn, 0)), flat(padq(qr, NOPE_DIM)),
                           flat(padq(_rope_partner(qr), NOPE_DIM))], axis=-1).astype(BF16)

    ukv = w_ukv.reshape(depth, KV_RANK, N_HEADS, NOPE_DIM + V_DIM)
    kn, vv = ukv[..., :NOPE_DIM], ukv[..., NOPE_DIM:]
    w_kv2 = jnp.concatenate([flat(padq(kn, 0)), vv.reshape(depth, KV_RANK, MLA_W)],
                            axis=-1).astype(BF16)
    return w_a, w_g, wq2, w_kv2


def kernel(x, c, ctx, c_ctx, ada_w, ada_b, norm1_g, norm2_g, w_in, q_norm_g, w_uq, kv_norm_g, w_ukv,
           lru_conv_w, lru_conv_b, lru_wa, lru_ba, lru_wi, lru_bi, lru_lambda, pool_w, pool_b,
           pool_scale, proj_mla, proj_lru, proj_pool, w_out, ffn_up, ffn_conv_w, ffn_conv_b,
           ffn_down, final_norm_g):
    nb, n_lat, _ = x.shape
    n_ctx = ctx.shape[1]
    depth = ada_w.shape[0]
    t_all = n_ctx + n_lat
    assert nb % 8 == 0 and n_ctx % T_CHUNK == 0 and n_lat % T_CHUNK == 0
    assert (n_ctx * nb) % ROWS_IN == 0 and (n_lat * nb) % ROWS_IN == 0

    xt = jnp.concatenate([ctx, x], axis=1).transpose(1, 0, 2).reshape(t_all * nb, D_MODEL)

    cond_rows = -(-(nb + 1) // 8) * 8
    cond = jnp.zeros((cond_rows, D_MODEL), F32).at[:nb].set(c).at[nb].set(c_ctx)
    mods = _ada_call(cond, ada_w, ada_b)
    mods = jnp.stack([jnp.broadcast_to(mods[:, nb:nb + 1], (depth, nb, N_MODS * D_MODEL)),
                      mods[:, :nb]], axis=1)

    w_a, w_g, wq2, w_kv2 = _pack_weights(w_in, w_uq, w_ukv)
    cos_t, sin_t = _rope_tables(n_ctx, n_lat, nb)
    row3 = lambda a: a.reshape(depth, 1, a.shape[-1])
    vec4 = lambda a: a.reshape(depth, 2, 1, a.shape[-1])
    g1, g2 = row3(norm1_g), row3(norm2_g)
    kv_g, q_g = row3(kv_norm_g), row3(q_norm_g)
    wa_b, wi_b = lru_wa.astype(BF16), lru_wi.astype(BF16)
    p_mla, p_lru, p_pool = proj_mla.astype(BF16), proj_lru.astype(BF16), proj_pool.astype(BF16)
    pool_wb, w_out_b = pool_w.astype(BF16), w_out.astype(BF16)
    w_val, w_gate = ffn_up[:, :, :D_FF].astype(BF16), ffn_up[:, :, D_FF:].astype(BF16)
    w_down = ffn_down.astype(BF16)

    dims = dict(nb=nb, n_ctx=n_ctx)
    for l in range(depth):
        q, k, v, ux, guy, pin = _inproj_call(l, xt, mods, g1, w_a, kv_g, w_kv2, q_g, wq2,
                                             cos_t, sin_t, **dims)
        att = _attn_call(q, k, v, **dims)
        hs = _lru_call(l, ux, lru_conv_w, row3(lru_conv_b), wa_b, wi_b, vec4(lru_ba),
                       vec4(lru_bi), vec4(lru_lambda), **dims)
        xt = _merge_call(l, xt, mods, g1, att, hs, guy, pin, w_g, p_mla, p_lru, p_pool, pool_wb,
                         row3(pool_b), row3(pool_scale), w_out_b, n_lat=n_lat, **dims)
        xt = _ffn_call(l, xt, mods, g2, w_val, w_gate, ffn_conv_w, row3(ffn_conv_b), w_down,
                       n_lat=n_lat, **dims)
    return _final_call(xt, final_norm_g, nb=nb, n_ctx=n_ctx, n_lat=n_lat)
```

```python
import functools

import jax
import jax.numpy as jnp
from jax import lax
from jax.experimental import pallas as pl
from jax.experimental.pallas import tpu as pltpu

F32 = jnp.float32
BF16 = jnp.bfloat16

D_MODEL = 1024
DEPTH = 4
GRID_W = 64
EPS = 1e-6
N_MODS = 6

N_HEADS = 8
Q_RANK = 256
KV_RANK = 128
NOPE_DIM = 64
ROPE_DIM = 32
V_DIM = 64
MLA_W = N_HEADS * V_DIM
SM_SCALE = (NOPE_DIM + ROPE_DIM) ** -0.5
ROPE_THETA = 10000.0

LRU_W = D_MODEL
LRU_BLOCKS = 8
LRU_BS = LRU_W // LRU_BLOCKS
LRU_CONV = 4
LRU_C = 8.0

POOL_WINDOWS = (2, 4, 8, 16)
POOL_W = D_MODEL // 2
POOL_G = POOL_W // len(POOL_WINDOWS)

D_FF = 2816
FFN_CONV = 3

COL_KV = KV_RANK
COL_KR = COL_KV + ROPE_DIM
COL_UX = COL_KR + LRU_W
COL_Q = COL_UX + Q_RANK
COL_UY = COL_Q + LRU_W
COL_POOL = COL_UY + POOL_W

LANE = 128
HEAD_PAD = LANE
PA_CKV, PA_KR, PA_KRP, PA_CQ = 0, 128, 256, 384
PA_ATT = PA_CQ + Q_RANK
PA_SEQ = 2 * LRU_W + POOL_W

ROWS_IN = 256
ROWS_MERGE = 256
ROWS_FFN = 256
Q_TILE = 256
T_CHUNK = 256
LRU_LANES = 128
POOL_HALO = max(POOL_WINDOWS) // 2
VMEM_LIMIT = 56 * 1024 * 1024


def _cparams(sem):
    return pltpu.CompilerParams(dimension_semantics=sem, vmem_limit_bytes=VMEM_LIMIT)


def _rms(x, g):
    return x * lax.rsqrt(jnp.mean(x * x, axis=-1, keepdims=True) + EPS) * g


def _modulate(y, shift, scale, nb):
    r, d = y.shape
    y3 = y.reshape(r // nb, nb, d)
    return (y3 * (1.0 + scale)[None] + shift[None]).reshape(r, d)


def _gate_rows(y, gate, nb):
    r, d = y.shape
    return (y.reshape(r // nb, nb, d) * gate[None]).reshape(r, d)


def _dot(a, b):
    return jnp.dot(a, b, preferred_element_type=F32)


def _ada_kernel(cond_ref, w_ref, b_ref, o_ref):
    c = cond_ref[...]
    s = (c * jax.nn.sigmoid(c)).astype(BF16)
    o_ref[...] = _dot(s, w_ref[...].astype(BF16)) + b_ref[...]


def _ada_call(cond, ada_w, ada_b):
    depth = ada_w.shape[0]
    rows = cond.shape[0]
    n_out = ada_w.shape[2]
    tn = n_out // 4
    return pl.pallas_call(
        _ada_kernel,
        out_shape=jax.ShapeDtypeStruct((depth, rows, n_out), F32),
        grid=(depth, n_out // tn),
        in_specs=[
            pl.BlockSpec((rows, D_MODEL), lambda l, j: (0, 0)),
            pl.BlockSpec((None, D_MODEL, tn), lambda l, j: (l, 0, j)),
            pl.BlockSpec((None, 1, tn), lambda l, j: (l, 0, j)),
        ],
        out_specs=pl.BlockSpec((None, rows, tn), lambda l, j: (l, 0, j)),
        compiler_params=_cparams(("parallel", "parallel")),
        name="ada",
    )(cond, ada_w, ada_b.reshape(depth, 1, n_out))


def _inproj_kernel(x_ref, mod_ref, g1_ref, perm_ref, watt_ref, wseq_ref, kvg_ref, wkv_ref, qg_ref,
                   wq_ref, cos_ref, sin_ref, q_ref, k_ref, v_ref, ux_ref, guy_ref, pin_ref, *, nb):
    x = x_ref[...]
    rows = x.shape[0]
    tt = rows // nb
    shift = mod_ref[:, 0:D_MODEL]
    scale = mod_ref[:, D_MODEL:2 * D_MODEL]
    h = _modulate(_rms(x, g1_ref[...]), shift, scale, nb).astype(BF16)

    ps = _dot(h, wseq_ref[...])
    ux_ref[...] = ps[:, 0:LRU_W]
    guy_ref[...] = jax.nn.gelu(ps[:, LRU_W:2 * LRU_W]).astype(guy_ref.dtype)
    pin_ref[...] = ps[:, 2 * LRU_W:]

    hp = _dot(perm_ref[...], h).astype(BF16)
    p = _dot(hp, watt_ref[...])
    cos = jnp.concatenate([cos_ref[...]] * nb, axis=0)
    sin = jnp.concatenate([sin_ref[...]] * nb, axis=0)

    ckvn = _rms(p[:, PA_CKV:PA_CKV + KV_RANK], kvg_ref[...]).astype(BF16)
    kv = _dot(ckvn, wkv_ref[...])
    kr = p[:, PA_KR:PA_KR + LANE] * cos + p[:, PA_KRP:PA_KRP + LANE] * sin
    hw = N_HEADS * HEAD_PAD
    k = (kv[:, :hw] + jnp.concatenate([kr] * N_HEADS, axis=1)).astype(k_ref.dtype)
    v = kv[:, hw:].astype(v_ref.dtype)

    cqn = _rms(p[:, PA_CQ:PA_CQ + Q_RANK], qg_ref[...]).astype(BF16)
    qq = _dot(cqn, wq_ref[...])
    cos_h = jnp.concatenate([cos] * N_HEADS, axis=1)
    sin_h = jnp.concatenate([sin] * N_HEADS, axis=1)
    q = ((qq[:, :hw] + qq[:, hw:2 * hw] * cos_h + qq[:, 2 * hw:] * sin_h) * SM_SCALE
         ).astype(q_ref.dtype)
    for b in range(nb):
        rs = slice(b * tt, (b + 1) * tt)
        q_ref[:, b * hw:(b + 1) * hw] = q[rs]
        k_ref[:, b * hw:(b + 1) * hw] = k[rs]
        v_ref[:, b * MLA_W:(b + 1) * MLA_W] = v[rs]


def _row_perm(rows, nb):
    tt = rows // nb
    dst = jnp.arange(rows)
    src = (dst % tt) * nb + dst // tt
    return (src[:, None] == jnp.arange(rows)[None, :]).astype(BF16)


def _inproj_call(l, xt, mods, g1, perm, w_att, w_seq, kv_g, w_kv2, q_g, wq2, cos_t, sin_t, *,
                 nb, n_ctx):
    rows = xt.shape[0]
    t_all = rows // nb
    r = ROWS_IN
    tt = r // nb
    ctx_tiles = (n_ctx * nb) // r
    seg = lambda i: jnp.where(i >= ctx_tiles, 1, 0)
    row_spec = lambda w: pl.BlockSpec((r, w), lambda i: (i, 0))
    pos_spec = lambda w: pl.BlockSpec((tt, w), lambda i: (i, 0))
    hw = N_HEADS * HEAD_PAD
    return pl.pallas_call(
        functools.partial(_inproj_kernel, nb=nb),
        out_shape=(
            jax.ShapeDtypeStruct((t_all, nb * hw), BF16),
            jax.ShapeDtypeStruct((t_all, nb * hw), BF16),
            jax.ShapeDtypeStruct((t_all, nb * MLA_W), BF16),
            jax.ShapeDtypeStruct((rows, LRU_W), F32),
            jax.ShapeDtypeStruct((rows, LRU_W), BF16),
            jax.ShapeDtypeStruct((rows, POOL_W), F32),
        ),
        grid=(rows // r,),
        in_specs=[
            row_spec(D_MODEL),
            pl.BlockSpec((None, None, nb, N_MODS * D_MODEL), lambda i: (l, seg(i), 0, 0)),
            pl.BlockSpec((None, 1, D_MODEL), lambda i: (l, 0, 0)),
            pl.BlockSpec((r, r), lambda i: (0, 0)),
            pl.BlockSpec((None, D_MODEL, PA_ATT), lambda i: (l, 0, 0)),
            pl.BlockSpec((None, D_MODEL, PA_SEQ), lambda i: (l, 0, 0)),
            pl.BlockSpec((None, 1, KV_RANK), lambda i: (l, 0, 0)),
            pl.BlockSpec((None, KV_RANK, hw + MLA_W), lambda i: (l, 0, 0)),
            pl.BlockSpec((None, 1, Q_RANK), lambda i: (l, 0, 0)),
            pl.BlockSpec((None, Q_RANK, 3 * hw), lambda i: (l, 0, 0)),
            pos_spec(LANE),
            pos_spec(LANE),
        ],
        out_specs=(pos_spec(nb * hw), pos_spec(nb * hw), pos_spec(nb * MLA_W), row_spec(LRU_W),
                   row_spec(LRU_W), row_spec(POOL_W)),
        compiler_params=_cparams(("parallel",)),
        name="inproj",
    )(xt, mods, g1, perm, w_att, w_seq, kv_g, w_kv2, q_g, wq2, cos_t, sin_t)


def _attn_kernel(q_ref, k_ref, v_ref, o_ref, *, n_ctx, n_all):
    i = pl.program_id(1)
    tq = q_ref.shape[0]
    lane = lax.broadcasted_iota(jnp.int32, (tq, LANE), 1)

    def run(nk):
        for pair in range(N_HEADS // 2):
            vp = v_ref[0:nk, pair * LANE:(pair + 1) * LANE]
            outs = []
            for hh in range(2):
                h = 2 * pair + hh
                qh = q_ref[:, h * HEAD_PAD:(h + 1) * HEAD_PAD]
                kh = k_ref[0:nk, h * HEAD_PAD:(h + 1) * HEAD_PAD]
                s = lax.dot_general(qh, kh, (((1,), (1,)), ((), ())),
                                    preferred_element_type=F32)
                m = jnp.max(s, axis=-1, keepdims=True)
                e = jnp.exp(s - m)
                den = jnp.sum(e, axis=-1, keepdims=True)
                outs.append(_dot(e.astype(BF16), vp) * (1.0 / den))
            o_ref[:, pair * LANE:(pair + 1) * LANE] = jnp.where(
                lane < V_DIM, outs[0], outs[1]).astype(o_ref.dtype)

    ctx_tiles = n_ctx // tq

    @pl.when(i < ctx_tiles)
    def _():
        run(n_ctx)

    @pl.when(i >= ctx_tiles)
    def _():
        run(n_all)


def _attn_call(q, k, v, *, nb, n_ctx):
    t_all = q.shape[0]
    hw = N_HEADS * HEAD_PAD
    return pl.pallas_call(
        functools.partial(_attn_kernel, n_ctx=n_ctx, n_all=t_all),
        out_shape=jax.ShapeDtypeStruct((t_all, nb * MLA_W), BF16),
        grid=(nb, t_all // Q_TILE),
        in_specs=[
            pl.BlockSpec((Q_TILE, hw), lambda b, i: (i, b)),
            pl.BlockSpec((t_all, hw), lambda b, i: (0, b)),
            pl.BlockSpec((t_all, MLA_W), lambda b, i: (0, b)),
        ],
        out_specs=pl.BlockSpec((Q_TILE, MLA_W), lambda b, i: (i, b)),
        compiler_params=_cparams(("parallel", "arbitrary")),
        name="attn",
    )(q, k, v)


def _chunk_of(d, j, n_chunks, ctx_chunks):
    bwd = jnp.where(j < ctx_chunks, ctx_chunks - 1 - j, n_chunks - 1 - (j - ctx_chunks))
    return jnp.where(d == 0, j, bwd)


def _lru_kernel(ux_ref, uxl_ref, uxr_ref, cw_ref, cb_ref, wa_ref, wi_ref, ba_ref, bi_ref,
                lam_ref, o_ref, a_s, b_s, h_s, *, nb, n_chunks, ctx_chunks):
    d = pl.program_id(1)
    j = pl.program_id(2)
    c = _chunk_of(d, j, n_chunks, ctx_chunks)
    rc = ux_ref.shape[0]
    tc = rc // nb

    seg_start = jnp.logical_or(c == 0, c == ctx_chunks)
    seg_end = jnp.logical_or(c == ctx_chunks - 1, c == n_chunks - 1)
    left = uxl_ref[...] * jnp.where(seg_start, 0.0, 1.0)
    right = uxr_ref[...] * jnp.where(seg_end, 0.0, 1.0)
    ext = jnp.concatenate([left, ux_ref[...], right], axis=0)
    u = (cw_ref[0:1, :] * ext[0:rc] + cw_ref[1:2, :] * ext[nb:nb + rc]
         + cw_ref[2:3, :] * ext[2 * nb:2 * nb + rc] + cw_ref[3:4, :] * ext[3 * nb:3 * nb + rc]
         + cb_ref[...])

    for blk in range(u.shape[1] // LRU_BS):
        sl = slice(blk * LRU_BS, (blk + 1) * LRU_BS)
        ub = u[:, sl]
        ubb = ub.astype(BF16)
        r = jax.nn.sigmoid(_dot(ubb, wa_ref[blk]) + ba_ref[:, sl])
        gi = jax.nn.sigmoid(_dot(ubb, wi_ref[blk]) + bi_ref[:, sl])
        z = -lam_ref[:, sl]
        softplus = jnp.maximum(z, 0.0) + jnp.log(1.0 + jnp.exp(-jnp.abs(z)))
        a = jnp.exp(-LRU_C * r * softplus)
        a_s[:, sl] = a
        b_s[:, sl] = jnp.sqrt(1.0 - a * a) * gi * ub

    @pl.when(j == 0)
    def _():
        h_s[...] = jnp.zeros_like(h_s)

    def step(s, h):
        t = jnp.where(d == 0, s, tc - 1 - s)
        r0 = pl.multiple_of(t * nb, nb)
        h = a_s[pl.ds(r0, nb), :] * h + b_s[pl.ds(r0, nb), :]
        o_ref[pl.ds(r0, nb), :] = h
        return h

    h_s[...] = lax.fori_loop(0, tc, step, h_s[...], unroll=8)


def _lru_call(l, ux, conv_w, conv_b, wa, wi, ba, bi, lam, *, nb, n_ctx):
    rows = ux.shape[0]
    t_all = rows // nb
    n_chunks = t_all // T_CHUNK
    ctx_chunks = n_ctx // T_CHUNK
    rc = T_CHUNK * nb
    lw = LRU_LANES
    nblk = lw // LRU_BS
    chunk = lambda d, j: _chunk_of(d, j, n_chunks, ctx_chunks)
    vec_spec = pl.BlockSpec((None, None, 1, lw), lambda n, d, j: (l, d, 0, n))
    return pl.pallas_call(
        functools.partial(_lru_kernel, nb=nb, n_chunks=n_chunks, ctx_chunks=ctx_chunks),
        out_shape=jax.ShapeDtypeStruct((2, rows, LRU_W), F32),
        grid=(LRU_W // lw, 2, n_chunks),
        in_specs=[
            pl.BlockSpec((rc, lw), lambda n, d, j: (chunk(d, j), n)),
            pl.BlockSpec((2 * nb, lw),
                         lambda n, d, j: (jnp.maximum(chunk(d, j) * (T_CHUNK // 2) - 1, 0), n)),
            pl.BlockSpec((nb, lw),
                         lambda n, d, j: (jnp.minimum((chunk(d, j) + 1) * T_CHUNK, t_all - 1), n)),
            pl.BlockSpec((None, LRU_CONV, lw), lambda n, d, j: (l, 0, n)),
            pl.BlockSpec((None, 1, lw), lambda n, d, j: (l, 0, n)),
            pl.BlockSpec((None, None, nblk, LRU_BS, LRU_BS), lambda n, d, j: (l, d, n, 0, 0)),
            pl.BlockSpec((None, None, nblk, LRU_BS, LRU_BS), lambda n, d, j: (l, d, n, 0, 0)),
            vec_spec, vec_spec, vec_spec,
        ],
        out_specs=pl.BlockSpec((None, rc, lw), lambda n, d, j: (d, chunk(d, j), n)),
        scratch_shapes=[pltpu.VMEM((rc, lw), F32), pltpu.VMEM((rc, lw), F32),
                        pltpu.VMEM((nb, lw), F32)],
        compiler_params=_cparams(("parallel", "arbitrary", "arbitrary")),
        name="lru",
    )(ux, ux, ux, conv_w, conv_b, wa, wi, ba, bi, lam)


def _pool_diff(ext, t0, nb, rows, n_ctx, n_lat):
    halo = POOL_HALO * nb

    def shifted_sum(arr, start, k):
        n = arr.shape[0]
        return arr[2 * k:n] + arr[0:n - 2 * k], start + k

    s2 = ext[nb:] + ext[:-nb]
    st = nb
    sums = {2: (s2, st)}
    cur, cst = s2, st
    step = nb
    for w in POOL_WINDOWS[1:]:
        cur, cst = shifted_sum(cur, cst, step)
        sums[w] = (cur, cst)
        step *= 2

    t = t0 + lax.broadcasted_iota(jnp.int32, (rows // nb, nb, POOL_G), 0).reshape(rows, POOL_G)
    in_ctx = t < n_ctx
    t_loc = jnp.where(in_ctx, t, t - n_ctx)
    n_seg = jnp.where(in_ctx, n_ctx, n_lat)
    outs = []
    for g, w in enumerate(POOL_WINDOWS):
        arr, start = sums[w]
        lanes = slice(g * POOL_G, (g + 1) * POOL_G)
        tot = arr[halo - start:halo - start + rows, lanes]
        left = w // 2
        right = w - 1 - left
        cnt = (jnp.minimum(t_loc + right + 1, n_seg) - jnp.maximum(t_loc - left, 0)).astype(F32)
        outs.append(tot / cnt - ext[halo:halo + rows, lanes])
    return outs


def _merge_kernel(x_ref, mod_ref, g1_ref, perm_ref, att_ref, hs_ref, guy_ref, pin_ref, pl_ref,
                  pr_ref, wg_ref, pm_ref, plru_ref, pp_ref, pw_ref, pb_ref, ps_ref, wo_ref, o_ref,
                  *, nb, n_ctx, n_lat):
    i = pl.program_id(0)
    rows = x_ref.shape[0]
    tt = rows // nb
    t0 = i * tt
    n_all = n_ctx + n_lat

    x = x_ref[...]
    shift = mod_ref[:, 0:D_MODEL]
    scale = mod_ref[:, D_MODEL:2 * D_MODEL]
    gate1 = mod_ref[:, 2 * D_MODEL:3 * D_MODEL]
    h = _modulate(_rms(x, g1_ref[...]), shift, scale, nb).astype(BF16)

    seg_start = jnp.logical_or(t0 == 0, t0 == n_ctx)
    seg_end = jnp.logical_or(t0 + tt == n_ctx, t0 + tt == n_all)
    ext = jnp.concatenate([pl_ref[...] * jnp.where(seg_start, 0.0, 1.0), pin_ref[...],
                           pr_ref[...] * jnp.where(seg_end, 0.0, 1.0)], axis=0)
    diffs = _pool_diff(ext, t0, nb, rows, n_ctx, n_lat)
    pooled = [_dot(dg.astype(BF16), pw_ref[g]) for g, dg in enumerate(diffs)]
    pool = (jnp.concatenate(pooled, axis=1) + pb_ref[...]) * ps_ref[...]

    rec = ((hs_ref[0] + hs_ref[1]) * guy_ref[...].astype(F32)).astype(BF16)

    def gate(k):
        return jax.nn.sigmoid(_dot(h, wg_ref[:, k * D_MODEL:(k + 1) * D_MODEL]))

    att_bt = jnp.concatenate([att_ref[:, b * MLA_W:(b + 1) * MLA_W] for b in range(nb)], axis=0)
    att = _dot(perm_ref[...], att_bt).astype(BF16)
    m = gate(0) * _dot(att, pm_ref[...])
    m = m + gate(1) * _dot(rec, plru_ref[...])
    m = m + gate(2) * _dot(pool.astype(BF16), pp_ref[...])
    y = _dot(m.astype(BF16), wo_ref[...])
    o_ref[...] = x + _gate_rows(y, gate1, nb)


def _merge_call(l, xt, mods, g1, perm_t, att, hs, guy, pin, w_g, p_mla, p_lru, p_pool, pool_w,
                pool_b, pool_scale, w_out, *, nb, n_ctx, n_lat):
    rows = xt.shape[0]
    r = ROWS_MERGE
    tt = r // nb
    hb = POOL_HALO * nb
    n_hblk = rows // hb
    ctx_tiles = (n_ctx * nb) // r
    seg = lambda i: jnp.where(i >= ctx_tiles, 1, 0)
    row_spec = lambda w: pl.BlockSpec((r, w), lambda i: (i, 0))
    wspec = lambda a, b: pl.BlockSpec((None, a, b), lambda i: (l, 0, 0))
    return pl.pallas_call(
        functools.partial(_merge_kernel, nb=nb, n_ctx=n_ctx, n_lat=n_lat),
        out_shape=jax.ShapeDtypeStruct((rows, D_MODEL), F32),
        grid=(rows // r,),
        in_specs=[
            row_spec(D_MODEL),
            pl.BlockSpec((None, None, nb, N_MODS * D_MODEL), lambda i: (l, seg(i), 0, 0)),
            wspec(1, D_MODEL),
            pl.BlockSpec((r, r), lambda i: (0, 0)),
            pl.BlockSpec((tt, nb * MLA_W), lambda i: (i, 0)),
            pl.BlockSpec((2, r, LRU_W), lambda i: (0, i, 0)),
            row_spec(LRU_W),
            row_spec(POOL_W),
            pl.BlockSpec((hb, POOL_W), lambda i: (jnp.maximum(i * (r // hb) - 1, 0), 0)),
            pl.BlockSpec((hb, POOL_W), lambda i: (jnp.minimum((i + 1) * (r // hb), n_hblk - 1), 0)),
            wspec(D_MODEL, 3 * D_MODEL),
            wspec(MLA_W, D_MODEL),
            wspec(LRU_W, D_MODEL),
            wspec(POOL_W, D_MODEL),
            pl.BlockSpec((None, len(POOL_WINDOWS), POOL_G, POOL_G), lambda i: (l, 0, 0, 0)),
            wspec(1, POOL_W),
            wspec(1, POOL_W),
            wspec(D_MODEL, D_MODEL),
        ],
        out_specs=row_spec(D_MODEL),
        compiler_params=_cparams(("parallel",)),
        name="merge",
    )(xt, mods, g1, perm_t, att, hs, guy, pin, pin, pin, w_g, p_mla, p_lru, p_pool, pool_w, pool_b,
      pool_scale, w_out)


def _ffn_kernel(x_ref, xl_ref, xr_ref, mod_ref, g2_ref, wv_ref, wgt_ref, cw_ref, cb_ref, wd_ref,
                o_ref, *, nb, n_ctx, n_lat):
    i = pl.program_id(0)
    rows = x_ref.shape[0]
    tt = rows // nb
    t0 = i * tt
    n_all = n_ctx + n_lat

    x = x_ref[...]
    shift = mod_ref[:, 3 * D_MODEL:4 * D_MODEL]
    scale = mod_ref[:, 4 * D_MODEL:5 * D_MODEL]
    gate2 = mod_ref[:, 5 * D_MODEL:6 * D_MODEL]
    g2 = g2_ref[...]

    xe = jnp.concatenate([xl_ref[...], x, xr_ref[...]], axis=0)
    he = _modulate(_rms(xe, g2), shift, scale, nb).astype(BF16)
    gate = _dot(he, wgt_ref[...])
    seg_start = jnp.logical_or(t0 == 0, t0 == n_ctx)
    seg_end = jnp.logical_or(t0 + tt == n_ctx, t0 + tt == n_all)
    g_prev = jnp.concatenate([gate[0:nb] * jnp.where(seg_start, 0.0, 1.0), gate[nb:rows]], axis=0)
    g_next = jnp.concatenate([gate[2 * nb:rows + nb],
                              gate[rows + nb:rows + 2 * nb] * jnp.where(seg_end, 0.0, 1.0)], axis=0)
    conv = (cw_ref[0:1, :] * g_prev + cw_ref[1:2, :] * gate[nb:nb + rows]
            + cw_ref[2:3, :] * g_next + cb_ref[...])
    val = _dot(he[nb:nb + rows], wv_ref[...])
    act = (conv * jax.nn.sigmoid(conv) * val).astype(BF16)
    y = _dot(act, wd_ref[...])
    o_ref[...] = x + _gate_rows(y, gate2, nb)


def _ffn_call(l, xt, mods, g2, w_val, w_gate, conv_w, conv_b, w_down, *, nb, n_ctx, n_lat):
    rows = xt.shape[0]
    t_all = rows // nb
    r = ROWS_FFN
    tt = r // nb
    ctx_tiles = (n_ctx * nb) // r
    seg = lambda i: jnp.where(i >= ctx_tiles, 1, 0)
    wspec = lambda a, b: pl.BlockSpec((None, a, b), lambda i: (l, 0, 0))
    return pl.pallas_call(
        functools.partial(_ffn_kernel, nb=nb, n_ctx=n_ctx, n_lat=n_lat),
        out_shape=jax.ShapeDtypeStruct((rows, D_MODEL), F32),
        grid=(rows // r,),
        in_specs=[
            pl.BlockSpec((r, D_MODEL), lambda i: (i, 0)),
            pl.BlockSpec((nb, D_MODEL), lambda i: (jnp.maximum(i * tt - 1, 0), 0)),
            pl.BlockSpec((nb, D_MODEL), lambda i: (jnp.minimum((i + 1) * tt, t_all - 1), 0)),
            pl.BlockSpec((None, None, nb, N_MODS * D_MODEL), lambda i: (l, seg(i), 0, 0)),
            wspec(1, D_MODEL),
            wspec(D_MODEL, D_FF),
            wspec(D_MODEL, D_FF),
            wspec(FFN_CONV, D_FF),
            wspec(1, D_FF),
            wspec(D_FF, D_MODEL),
        ],
        out_specs=pl.BlockSpec((r, D_MODEL), lambda i: (i, 0)),
        compiler_params=_cparams(("parallel",)),
        name="ffn",
    )(xt, xt, xt, mods, g2, w_val, w_gate, conv_w, conv_b, w_down)


def _final_kernel(x_ref, g_ref, o_ref):
    o_ref[...] = _rms(x_ref[...], g_ref[...])


def _final_call(xt, g, *, nb, n_ctx, n_lat):
    t_all = n_ctx + n_lat
    tq = Q_TILE
    off = n_ctx // tq
    return pl.pallas_call(
        _final_kernel,
        out_shape=jax.ShapeDtypeStruct((nb, n_lat, D_MODEL), F32),
        grid=(nb, n_lat // tq),
        in_specs=[
            pl.BlockSpec((tq, D_MODEL), lambda b, i: (i + off, b)),
            pl.BlockSpec((1, D_MODEL), lambda b, i: (0, 0)),
        ],
        out_specs=pl.BlockSpec((None, tq, D_MODEL), lambda b, i: (b, i, 0)),
        compiler_params=_cparams(("parallel", "parallel")),
        name="final_norm",
    )(xt.reshape(t_all, nb * D_MODEL), g.reshape(1, D_MODEL))


def _rope_tables(n_ctx, n_lat):
    half = ROPE_DIM // 2
    quarter = half // 2
    inv = ROPE_THETA ** (-jnp.arange(0, half, 2, dtype=F32) / half)
    t = jnp.arange(n_lat)
    ang_r = (t // GRID_W).astype(F32)[:, None] * inv
    ang_c = (t % GRID_W).astype(F32)[:, None] * inv
    cr, sr, cc, sc = jnp.cos(ang_r), jnp.sin(ang_r), jnp.cos(ang_c), jnp.sin(ang_c)
    cos32 = jnp.concatenate([cr, cr, cc, cc], axis=1)
    sin32 = jnp.concatenate([-sr, sr, -sc, sc], axis=1)
    assert cos32.shape[1] == 4 * quarter == ROPE_DIM
    cos32 = jnp.concatenate([jnp.ones((n_ctx, ROPE_DIM), F32), cos32], axis=0)
    sin32 = jnp.concatenate([jnp.zeros((n_ctx, ROPE_DIM), F32), sin32], axis=0)
    pad = lambda a: jnp.pad(a, ((0, 0), (NOPE_DIM, HEAD_PAD - NOPE_DIM - ROPE_DIM)))
    return pad(cos32), pad(sin32)


def _rope_partner(w):
    q = ROPE_DIM // 4
    return jnp.concatenate([w[..., q:2 * q], w[..., 0:q], w[..., 3 * q:4 * q], w[..., 2 * q:3 * q]],
                           axis=-1)


def _pack_weights(w_in, w_uq, w_ukv):
    depth = w_in.shape[0]
    zeros = lambda n: jnp.zeros((depth, D_MODEL, n), w_in.dtype)
    w_kr = w_in[:, :, COL_KV:COL_KR]
    tail = HEAD_PAD - NOPE_DIM - ROPE_DIM
    kr_grp = lambda w: jnp.concatenate([zeros(NOPE_DIM), w, zeros(tail)], axis=-1)
    w_att = jnp.concatenate([
        w_in[:, :, :COL_KV], kr_grp(w_kr), kr_grp(_rope_partner(w_kr)),
        w_in[:, :, COL_UX:COL_Q]], axis=-1).astype(BF16)
    w_seq = jnp.concatenate([w_in[:, :, COL_KR:COL_UX], w_in[:, :, COL_Q:COL_UY],
                             w_in[:, :, COL_UY:COL_POOL]], axis=-1).astype(BF16)
    w_g = w_in[:, :, COL_POOL:].astype(BF16)

    uq = w_uq.reshape(depth, Q_RANK, N_HEADS, NOPE_DIM + ROPE_DIM)
    qn, qr = uq[..., :NOPE_DIM], uq[..., NOPE_DIM:]
    padq = lambda a, lo: jnp.pad(a, ((0, 0), (0, 0), (0, 0), (lo, HEAD_PAD - lo - a.shape[-1])))
    flat = lambda a: a.reshape(depth, a.shape[1], N_HEADS * HEAD_PAD)
    wq2 = jnp.concatenate([flat(padq(qn, 0)), flat(padq(qr, NOPE_DIM)),
                           flat(padq(_rope_partner(qr), NOPE_DIM))], axis=-1).astype(BF16)

    ukv = w_ukv.reshape(depth, KV_RANK, N_HEADS, NOPE_DIM + V_DIM)
    kn, vv = ukv[..., :NOPE_DIM], ukv[..., NOPE_DIM:]
    w_kv2 = jnp.concatenate([flat(padq(kn, 0)), vv.reshape(depth, KV_RANK, MLA_W)],
                            axis=-1).astype(BF16)
    return w_att, w_seq, w_g, wq2, w_kv2


def kernel(x, c, ctx, c_ctx, ada_w, ada_b, norm1_g, norm2_g, w_in, q_norm_g, w_uq, kv_norm_g, w_ukv,
           lru_conv_w, lru_conv_b, lru_wa, lru_ba, lru_wi, lru_bi, lru_lambda, pool_w, pool_b,
           pool_scale, proj_mla, proj_lru, proj_pool, w_out, ffn_up, ffn_conv_w, ffn_conv_b,
           ffn_down, final_norm_g):
    nb, n_lat, _ = x.shape
    n_ctx = ctx.shape[1]
    depth = ada_w.shape[0]
    t_all = n_ctx + n_lat
    assert nb % 8 == 0 and n_ctx % T_CHUNK == 0 and n_lat % T_CHUNK == 0
    assert (n_ctx * nb) % ROWS_IN == 0 and (n_lat * nb) % ROWS_IN == 0

    xt = jnp.concatenate([ctx, x], axis=1).transpose(1, 0, 2).reshape(t_all * nb, D_MODEL)

    cond_rows = -(-(nb + 1) // 8) * 8
    cond = jnp.zeros((cond_rows, D_MODEL), F32).at[:nb].set(c).at[nb].set(c_ctx)
    mods = _ada_call(cond, ada_w, ada_b)
    mods = jnp.stack([jnp.broadcast_to(mods[:, nb:nb + 1], (depth, nb, N_MODS * D_MODEL)),
                      mods[:, :nb]], axis=1)

    w_att, w_seq, w_g, wq2, w_kv2 = _pack_weights(w_in, w_uq, w_ukv)
    cos_t, sin_t = _rope_tables(n_ctx, n_lat)
    perm_in = _row_perm(ROWS_IN, nb)
    perm_merge_t = _row_perm(ROWS_MERGE, nb).T
    row3 = lambda a: a.reshape(depth, 1, a.shape[-1])
    vec4 = lambda a: a.reshape(depth, 2, 1, a.shape[-1])
    g1, g2 = row3(norm1_g), row3(norm2_g)
    kv_g, q_g = row3(kv_norm_g), row3(q_norm_g)
    wa_b, wi_b = lru_wa.astype(BF16), lru_wi.astype(BF16)
    p_mla, p_lru, p_pool = proj_mla.astype(BF16), proj_lru.astype(BF16), proj_pool.astype(BF16)
    pool_wb, w_out_b = pool_w.astype(BF16), w_out.astype(BF16)
    w_val, w_gate = ffn_up[:, :, :D_FF].astype(BF16), ffn_up[:, :, D_FF:].astype(BF16)
    w_down = ffn_down.astype(BF16)

    dims = dict(nb=nb, n_ctx=n_ctx)
    for l in range(depth):
        q, k, v, ux, guy, pin = _inproj_call(l, xt, mods, g1, perm_in, w_att, w_seq, kv_g, w_kv2,
                                             q_g, wq2, cos_t, sin_t, **dims)
        att = _attn_call(q, k, v, **dims)
        hs = _lru_call(l, ux, lru_conv_w, row3(lru_conv_b), wa_b, wi_b, vec4(lru_ba),
                       vec4(lru_bi), vec4(lru_lambda), **dims)
        xt = _merge_call(l, xt, mods, g1, perm_merge_t, att, hs, guy, pin, w_g, p_mla, p_lru,
                         p_pool, pool_wb,
                         row3(pool_b), row3(pool_scale), w_out_b, n_lat=n_lat, **dims)
        xt = _ffn_call(l, xt, mods, g2, w_val, w_gate, ffn_conv_w, row3(ffn_conv_b), w_down,
                       n_lat=n_lat, **dims)
    return _final_call(xt, final_norm_g, nb=nb, n_ctx=n_ctx, n_lat=n_lat)
```

```python
import functools
import math

import jax
import jax.numpy as jnp
from jax import lax
from jax.experimental import pallas as pl
from jax.experimental.pallas import tpu as pltpu

F32 = jnp.float32
BF16 = jnp.bfloat16

D_MODEL = 1024
GRID_W = 64
EPS = 1e-6
N_MODS = 6

N_HEADS = 8
Q_RANK = 256
KV_RANK = 128
NOPE_DIM = 64
ROPE_DIM = 32
V_DIM = 64
MLA_W = N_HEADS * V_DIM
SM_SCALE = (NOPE_DIM + ROPE_DIM) ** -0.5
ROPE_THETA = 10000.0
LOG2E = math.log2(math.e)

LRU_W = D_MODEL
LRU_BLOCKS = 8
LRU_BS = LRU_W // LRU_BLOCKS
LRU_CONV = 4
LRU_LEFT = LRU_CONV // 2
LRU_RIGHT = LRU_CONV - 1 - LRU_LEFT
LRU_C = 8.0

POOL_WINDOWS = (2, 4, 8, 16)
POOL_W = D_MODEL // 2
POOL_G = POOL_W // len(POOL_WINDOWS)

D_FF = 2816
FFN_CONV = 3

COL_KV = KV_RANK
COL_KR = COL_KV + ROPE_DIM
COL_UX = COL_KR + LRU_W
COL_Q = COL_UX + Q_RANK
COL_UY = COL_Q + LRU_W
COL_POOL = COL_UY + POOL_W

LANE = 128
HEAD_PAD = LANE
HEADS_W = N_HEADS * HEAD_PAD
PA_CKV, PA_KR, PA_KRP, PA_CQ = 0, 128, 256, 384
PA_ATT = PA_CQ + Q_RANK
PA_SEQ = 2 * LRU_W + POOL_W

ROWS_PERM = 256
ROWS_IN = 512
ROWS_MERGE = 512
ROWS_FFN = 512
ROWS_OUT = 512
Q_TILE = 512
POOL_HALO = max(POOL_WINDOWS) // 2
VMEM_LIMIT = 56 * 1024 * 1024


def _cparams(sem):
    return pltpu.CompilerParams(dimension_semantics=sem, vmem_limit_bytes=VMEM_LIMIT)


def _resident(block_shape, index_map):
    return pl.BlockSpec(block_shape, index_map, pipeline_mode=pl.Buffered(1))


def _rms(x, g):
    return x * lax.rsqrt(jnp.mean(x * x, axis=-1, keepdims=True) + EPS) * g


def _modulate(y, shift, scale, nb):
    r, d = y.shape
    y3 = y.reshape(r // nb, nb, d)
    return (y3 * (1.0 + scale)[None] + shift[None]).reshape(r, d)


def _gate_rows(y, gate, nb):
    r, d = y.shape
    return (y.reshape(r // nb, nb, d) * gate[None]).reshape(r, d)


def _dot(a, b):
    return jnp.dot(a, b, preferred_element_type=F32)


def _row_perm(rows, nb):
    tt = rows // nb
    dst = jnp.arange(rows)
    src = (dst % tt) * nb + dst // tt
    return (src[:, None] == jnp.arange(rows)[None, :]).astype(BF16)


def _tile_of(j, lat_tiles, n_tiles, reverse):
    ctx_tiles = n_tiles - lat_tiles
    if reverse:
        return jnp.where(j < ctx_tiles, n_tiles - 1 - j, lat_tiles - 1 - (j - ctx_tiles))
    return jnp.where(j < ctx_tiles, lat_tiles + j, j - ctx_tiles)


def _segment(t0, tt, n_lat, n_all):
    seg_start = jnp.logical_or(t0 == 0, t0 == n_lat)
    seg_end = jnp.logical_or(t0 + tt == n_lat, t0 + tt == n_all)
    return jnp.where(seg_start, 0.0, 1.0), jnp.where(seg_end, 0.0, 1.0)


def _ada_kernel(cond_ref, w_ref, b_ref, o_ref):
    c = cond_ref[...]
    s = (c * jax.nn.sigmoid(c)).astype(BF16)
    o_ref[...] = _dot(s, w_ref[...].astype(BF16)) + b_ref[...]


def _ada_call(cond, ada_w, ada_b):
    depth = ada_w.shape[0]
    rows = cond.shape[0]
    n_out = ada_w.shape[2]
    tn = n_out // 4
    return pl.pallas_call(
        _ada_kernel,
        out_shape=jax.ShapeDtypeStruct((depth, rows, n_out), F32),
        grid=(depth, n_out // tn),
        in_specs=[
            pl.BlockSpec((rows, D_MODEL), lambda l, j: (0, 0)),
            pl.BlockSpec((None, D_MODEL, tn), lambda l, j: (l, 0, j)),
            pl.BlockSpec((None, 1, tn), lambda l, j: (l, 0, j)),
        ],
        out_specs=pl.BlockSpec((None, rows, tn), lambda l, j: (l, 0, j)),
        compiler_params=_cparams(("parallel", "parallel")),
        name="ada",
    )(cond, ada_w, ada_b.reshape(depth, 1, n_out))


def _lru_coeffs(u_of, wa_ref, wi_ref, ba_ref, bi_ref, lam_ref, a_s, b_s):
    for blk in range(LRU_BLOCKS):
        sl = slice(blk * LRU_BS, (blk + 1) * LRU_BS)
        u = u_of(sl)
        ub = u.astype(BF16)
        ta = jnp.tanh(_dot(ub, wa_ref[blk]) + ba_ref[:, sl])
        ti = jnp.tanh(_dot(ub, wi_ref[blk]) + bi_ref[:, sl])
        z = -lam_ref[:, sl]
        softplus = jnp.maximum(z, 0.0) + jnp.log(1.0 + jnp.exp(-jnp.abs(z)))
        ch = (-0.5 * LRU_C * LOG2E) * softplus
        a = jnp.exp2(ta * ch + ch)
        y = 1.0 - a * a
        root = jnp.where(y > 0.0, y * lax.rsqrt(y), 0.0)
        uh = 0.5 * u
        a_s[:, sl] = a
        b_s[:, sl] = (ti * uh + uh) * root


def _lru_gate_specs(l, d):
    vec = _resident((None, None, 1, LRU_W), lambda j: (l, d, 0, 0))
    mat = _resident((None, None, LRU_BLOCKS, LRU_BS, LRU_BS), lambda j: (l, d, 0, 0, 0))
    return [mat, mat, vec, vec, vec]


def _inproj_kernel(x_ref, xr_ref, mod_ref, g1_ref, perm_ref, watt_ref, wseq_ref, kvg_ref, wkv_ref,
                   qg_ref, wq_ref, cos_ref, sin_ref, vone_ref, cw_ref, cb_ref, wa_ref, wi_ref,
                   ba_ref, bi_ref, lam_ref, q_ref, k_ref, v_ref, u_ref, guy_ref, pin_ref, hf_ref,
                   a_s, b_s, h_s, tail_s, *, nb, n_lat, n_ctx):
    j = pl.program_id(0)
    rows = x_ref.shape[0]
    tt = rows // nb
    n_all = n_lat + n_ctx
    tile = _tile_of(j, n_lat // tt, n_all // tt, False)
    keep_l, keep_r = _segment(tile * tt, tt, n_lat, n_all)
    n_sub = rows // ROWS_PERM
    ts = ROWS_PERM // nb
    shift = mod_ref[:, 0:D_MODEL]
    scale = mod_ref[:, D_MODEL:2 * D_MODEL]
    norm_mod = lambda v: _modulate(_rms(v, g1_ref[...]), shift, scale, nb)

    @pl.when(j == 0)
    def _():
        h_s[...] = jnp.zeros_like(h_s)
        tail_s[...] = jnp.zeros_like(tail_s)

    h = norm_mod(x_ref[...]).astype(BF16)

    h_next = (norm_mod(xr_ref[...]) * keep_r).astype(BF16)
    ux_ext = _dot(jnp.concatenate([h, h_next], axis=0), wseq_ref[:, 0:LRU_W])
    guy_ref[...] = jax.nn.gelu(_dot(h, wseq_ref[:, LRU_W:2 * LRU_W])).astype(guy_ref.dtype)
    pin_ref[...] = _dot(h, wseq_ref[:, 2 * LRU_W:])

    ext = jnp.concatenate([tail_s[...] * keep_l, ux_ext], axis=0)
    tail_s[...] = ux_ext[rows - LRU_LEFT * nb:rows]

    def conv(sl):
        u = cb_ref[:, sl]
        for tap in range(LRU_CONV):
            u = u + cw_ref[tap:tap + 1, sl] * ext[tap * nb:tap * nb + rows, sl]
        u_ref[:, sl] = u
        return u

    _lru_coeffs(conv, wa_ref, wi_ref, ba_ref, bi_ref, lam_ref, a_s, b_s)

    perm = perm_ref[...]
    hp = jnp.concatenate([_dot(perm, h[s * ROWS_PERM:(s + 1) * ROWS_PERM]).astype(BF16)
                          for s in range(n_sub)], axis=0)
    p = _dot(hp, watt_ref[...])
    tile_pos = lambda ref: jnp.concatenate(
        [ref[s * ts:(s + 1) * ts] for s in range(n_sub) for _ in range(nb)], axis=0)
    cos = tile_pos(cos_ref)
    sin = tile_pos(sin_ref)

    ckvn = _rms(p[:, PA_CKV:PA_CKV + KV_RANK], kvg_ref[...]).astype(BF16)
    kv = _dot(ckvn, wkv_ref[...])
    kr = p[:, PA_KR:PA_KR + LANE] * cos + p[:, PA_KRP:PA_KRP + LANE] * sin
    k = (kv[:, :HEADS_W] + jnp.concatenate([kr] * N_HEADS, axis=1)).astype(k_ref.dtype)
    v = (kv[:, HEADS_W:] + vone_ref[...]).astype(v_ref.dtype)

    cqn = _rms(p[:, PA_CQ:PA_CQ + Q_RANK], qg_ref[...]).astype(BF16)
    qq = _dot(cqn, wq_ref[...])
    cos_h = jnp.concatenate([cos] * N_HEADS, axis=1)
    sin_h = jnp.concatenate([sin] * N_HEADS, axis=1)
    q = ((qq[:, :HEADS_W] + qq[:, HEADS_W:2 * HEADS_W] * cos_h + qq[:, 2 * HEADS_W:] * sin_h)
         * (SM_SCALE * LOG2E)).astype(q_ref.dtype)
    for s in range(n_sub):
        for b in range(nb):
            rs = slice(s * ROWS_PERM + b * ts, s * ROWS_PERM + (b + 1) * ts)
            ps = slice(s * ts, (s + 1) * ts)
            cs = slice(b * HEADS_W, (b + 1) * HEADS_W)
            q_ref[ps, cs] = q[rs]
            k_ref[ps, cs] = k[rs]
            v_ref[ps, cs] = v[rs]

    def step(t, hc):
        rs = pl.ds(pl.multiple_of(t * nb, nb), nb)
        hc = a_s[rs, :] * hc + b_s[rs, :]
        hf_ref[rs, :] = hc.astype(hf_ref.dtype)
        return hc

    h_s[...] = lax.fori_loop(0, tt, step, h_s[...], unroll=8)


def _inproj_call(l, xt, mods, g1, perm, w_att, w_seq, kv_g, w_kv2, q_g, wq2, cos_t, sin_t, v_one,
                 lru_params, *, nb, n_lat, n_ctx):
    rows = xt.shape[0]
    t_all = rows // nb
    r = ROWS_IN
    tt = r // nb
    n_tiles = rows // r
    lat_tiles = (n_lat * nb) // r
    tile = lambda j: _tile_of(j, lat_tiles, n_tiles, False)
    seg = lambda j: jnp.where(tile(j) >= lat_tiles, 1, 0)
    row_spec = lambda w: pl.BlockSpec((r, w), lambda j: (tile(j), 0))
    pos_spec = lambda w: pl.BlockSpec((tt, w), lambda j: (tile(j), 0))
    wspec = lambda a, b: _resident((None, a, b), lambda j: (l, 0, 0))
    return pl.pallas_call(
        functools.partial(_inproj_kernel, nb=nb, n_lat=n_lat, n_ctx=n_ctx),
        out_shape=(
            jax.ShapeDtypeStruct((t_all, nb * HEADS_W), BF16),
            jax.ShapeDtypeStruct((t_all, nb * HEADS_W), BF16),
            jax.ShapeDtypeStruct((t_all, nb * HEADS_W), BF16),
            jax.ShapeDtypeStruct((rows, LRU_W), F32),
            jax.ShapeDtypeStruct((rows, LRU_W), BF16),
            jax.ShapeDtypeStruct((rows, POOL_W), F32),
            jax.ShapeDtypeStruct((rows, LRU_W), BF16),
        ),
        grid=(n_tiles,),
        in_specs=[
            row_spec(D_MODEL),
            pl.BlockSpec((nb, D_MODEL), lambda j: (jnp.minimum((tile(j) + 1) * tt, t_all - 1), 0)),
            pl.BlockSpec((None, None, nb, N_MODS * D_MODEL), lambda j: (l, seg(j), 0, 0)),
            wspec(1, D_MODEL),
            _resident((ROWS_PERM, ROWS_PERM), lambda j: (0, 0)),
            wspec(D_MODEL, PA_ATT),
            wspec(D_MODEL, PA_SEQ),
            wspec(1, KV_RANK),
            wspec(KV_RANK, 2 * HEADS_W),
            wspec(1, Q_RANK),
            wspec(Q_RANK, 3 * HEADS_W),
            pos_spec(LANE),
            pos_spec(LANE),
            _resident((1, HEADS_W), lambda j: (0, 0)),
            _resident((None, LRU_CONV, LRU_W), lambda j: (l, 0, 0)),
            _resident((None, 1, LRU_W), lambda j: (l, 0, 0)),
        ] + _lru_gate_specs(l, 0),
        out_specs=(pos_spec(nb * HEADS_W), pos_spec(nb * HEADS_W), pos_spec(nb * HEADS_W),
                   row_spec(LRU_W), row_spec(LRU_W), row_spec(POOL_W), row_spec(LRU_W)),
        scratch_shapes=[pltpu.VMEM((r, LRU_W), F32), pltpu.VMEM((r, LRU_W), F32),
                        pltpu.VMEM((nb, LRU_W), F32), pltpu.VMEM((LRU_LEFT * nb, LRU_W), F32)],
        compiler_params=_cparams(("arbitrary",)),
        name="inproj",
    )(xt, xt, mods, g1, perm, w_att, w_seq, kv_g, w_kv2, q_g, wq2, cos_t, sin_t, v_one, *lru_params)


def _attn_kernel(q_ref, k_ref, v_ref, o_ref):
    tq = q_ref.shape[0]
    lane = lax.broadcasted_iota(jnp.int32, (tq, LANE), 1)
    for pair in range(N_HEADS // 2):
        res = []
        for hh in range(2):
            hs = slice((2 * pair + hh) * HEAD_PAD, (2 * pair + hh + 1) * HEAD_PAD)
            s = lax.dot_general(q_ref[:, hs], k_ref[:, hs], (((1,), (1,)), ((), ())),
                                preferred_element_type=F32)
            e = jnp.exp2(s - jnp.max(s, axis=-1, keepdims=True))
            res.append(_dot(e.astype(BF16), v_ref[:, hs]))
        den0 = res[0][:, V_DIM:V_DIM + 1]
        den1 = res[1][:, 0:1]
        o_ref[:, pair * LANE:(pair + 1) * LANE] = jnp.where(
            lane < V_DIM, res[0] * (1.0 / den0), res[1] * (1.0 / den1)).astype(o_ref.dtype)


def _attn_call(q, k, v, *, nb, q_tile, q_off, n_q, k_rows, k_off):
    qo, ko = q_off // q_tile, k_off // k_rows
    return pl.pallas_call(
        _attn_kernel,
        out_shape=jax.ShapeDtypeStruct((n_q, nb * MLA_W), BF16),
        grid=(nb, n_q // q_tile),
        in_specs=[
            pl.BlockSpec((q_tile, HEADS_W), lambda b, i: (i + qo, b)),
            pl.BlockSpec((k_rows, HEADS_W), lambda b, i: (ko, b)),
            pl.BlockSpec((k_rows, HEADS_W), lambda b, i: (ko, b)),
        ],
        out_specs=pl.BlockSpec((q_tile, MLA_W), lambda b, i: (i, b)),
        compiler_params=_cparams(("parallel", "arbitrary")),
        name="attn",
    )(q, k, v)


def _pool_diff(ext, t0, nb, rows, n_lat, n_ctx):
    halo = POOL_HALO * nb

    def shifted_sum(arr, start, k):
        n = arr.shape[0]
        return arr[2 * k:n] + arr[0:n - 2 * k], start + k

    cur, cst = ext[nb:] + ext[:-nb], nb
    sums = {POOL_WINDOWS[0]: (cur, cst)}
    step = nb
    for w in POOL_WINDOWS[1:]:
        cur, cst = shifted_sum(cur, cst, step)
        sums[w] = (cur, cst)
        step *= 2

    t = t0 + lax.broadcasted_iota(jnp.int32, (rows // nb, nb, POOL_G), 0).reshape(rows, POOL_G)
    in_lat = t < n_lat
    t_loc = jnp.where(in_lat, t, t - n_lat)
    n_seg = jnp.where(in_lat, n_lat, n_ctx)
    outs = []
    for g, w in enumerate(POOL_WINDOWS):
        arr, start = sums[w]
        lanes = slice(g * POOL_G, (g + 1) * POOL_G)
        tot = arr[halo - start:halo - start + rows, lanes]
        left = w // 2
        right = w - 1 - left
        cnt = (jnp.minimum(t_loc + right + 1, n_seg) - jnp.maximum(t_loc - left, 0)).astype(F32)
        outs.append(tot / cnt - ext[halo:halo + rows, lanes])
    return outs


def _merge_kernel(x_ref, mod_ref, g1_ref, perm_ref, attl_ref, attc_ref, hf_ref, guy_ref, u_ref,
                  pin_ref, pl_ref, pr_ref, wa_ref, wi_ref, ba_ref, bi_ref, lam_ref, wg_ref, pm_ref,
                  plru_ref, pp_ref, pw_ref, pb_ref, ps_ref, wo_ref, o_ref, a_s, b_s, h_s, rec_s,
                  *, nb, n_lat, n_ctx):
    j = pl.program_id(0)
    rows = x_ref.shape[0]
    tt = rows // nb
    n_all = n_lat + n_ctx
    lat_tiles = n_lat // tt
    tile = _tile_of(j, lat_tiles, n_all // tt, True)
    t0 = tile * tt
    keep_l, keep_r = _segment(t0, tt, n_lat, n_all)

    @pl.when(j == 0)
    def _():
        h_s[...] = jnp.zeros_like(h_s)

    _lru_coeffs(lambda sl: u_ref[:, sl], wa_ref, wi_ref, ba_ref, bi_ref, lam_ref, a_s, b_s)

    x = x_ref[...]
    shift = mod_ref[:, 0:D_MODEL]
    scale = mod_ref[:, D_MODEL:2 * D_MODEL]
    gate1 = mod_ref[:, 2 * D_MODEL:3 * D_MODEL]
    h = _modulate(_rms(x, g1_ref[...]), shift, scale, nb).astype(BF16)

    pext = jnp.concatenate([pl_ref[...] * keep_l, pin_ref[...], pr_ref[...] * keep_r], axis=0)
    diffs = _pool_diff(pext, t0, nb, rows, n_lat, n_ctx)
    pooled = [_dot(dg.astype(BF16), pw_ref[g]) for g, dg in enumerate(diffs)]
    pool = (jnp.concatenate(pooled, axis=1) + pb_ref[...]) * ps_ref[...]

    def gate(kk):
        return 0.5 * jnp.tanh(_dot(h, wg_ref[:, kk * D_MODEL:(kk + 1) * D_MODEL])) + 0.5

    is_ctx = jnp.where(tile >= lat_tiles, 1.0, 0.0)
    att_pos = (attl_ref[...].astype(F32) * (1.0 - is_ctx)
               + attc_ref[...].astype(F32) * is_ctx).astype(BF16)
    ts = ROWS_PERM // nb
    perm_t = perm_ref[...]
    att = jnp.concatenate([
        _dot(perm_t, jnp.concatenate([att_pos[s * ts:(s + 1) * ts, b * MLA_W:(b + 1) * MLA_W]
                                      for b in range(nb)], axis=0)).astype(BF16)
        for s in range(rows // ROWS_PERM)], axis=0)
    m = gate(0) * _dot(att, pm_ref[...]) + gate(2) * _dot(pool.astype(BF16), pp_ref[...])
    g_rec = gate(1)

    def step(s, hc):
        rs = pl.ds(pl.multiple_of((tt - 1 - s) * nb, nb), nb)
        hc = a_s[rs, :] * hc + b_s[rs, :]
        rec_s[rs, :] = ((hf_ref[rs, :].astype(F32) + hc)
                        * guy_ref[rs, :].astype(F32)).astype(rec_s.dtype)
        return hc

    h_s[...] = lax.fori_loop(0, tt, step, h_s[...], unroll=8)

    m = m + g_rec * _dot(rec_s[...], plru_ref[...])
    y = _dot(m.astype(BF16), wo_ref[...])
    o_ref[...] = x + _gate_rows(y, gate1, nb)


def _merge_call(l, xt, mods, g1, perm_t, att_l, att_c, hf, guy, u, pin, gate_params, w_g, p_mla,
                p_lru, p_pool, pool_w, pool_b, pool_scale, w_out, *, nb, n_lat, n_ctx):
    rows = xt.shape[0]
    r = ROWS_MERGE
    tt = r // nb
    hb = POOL_HALO * nb
    n_tiles = rows // r
    lat_tiles = (n_lat * nb) // r
    tile = lambda j: _tile_of(j, lat_tiles, n_tiles, True)
    seg = lambda j: jnp.where(tile(j) >= lat_tiles, 1, 0)
    row_spec = lambda w: pl.BlockSpec((r, w), lambda j: (tile(j), 0))
    wspec = lambda a, b: _resident((None, a, b), lambda j: (l, 0, 0))
    return pl.pallas_call(
        functools.partial(_merge_kernel, nb=nb, n_lat=n_lat, n_ctx=n_ctx),
        out_shape=jax.ShapeDtypeStruct((rows, D_MODEL), F32),
        grid=(n_tiles,),
        in_specs=[
            row_spec(D_MODEL),
            pl.BlockSpec((None, None, nb, N_MODS * D_MODEL), lambda j: (l, seg(j), 0, 0)),
            wspec(1, D_MODEL),
            _resident((ROWS_PERM, ROWS_PERM), lambda j: (0, 0)),
            pl.BlockSpec((tt, nb * MLA_W), lambda j: (jnp.minimum(tile(j), lat_tiles - 1), 0)),
            pl.BlockSpec((tt, nb * MLA_W), lambda j: (jnp.maximum(tile(j) - lat_tiles, 0), 0)),
            row_spec(LRU_W),
            row_spec(LRU_W),
            row_spec(LRU_W),
            row_spec(POOL_W),
            pl.BlockSpec((hb, POOL_W), lambda j: (jnp.maximum(tile(j) * (r // hb) - 1, 0), 0)),
            pl.BlockSpec((hb, POOL_W),
                         lambda j: (jnp.minimum((tile(j) + 1) * (r // hb), rows // hb - 1), 0)),
        ] + _lru_gate_specs(l, 1) + [
            wspec(D_MODEL, 3 * D_MODEL),
            wspec(MLA_W, D_MODEL),
            wspec(LRU_W, D_MODEL),
            wspec(POOL_W, D_MODEL),
            _resident((None, len(POOL_WINDOWS), POOL_G, POOL_G), lambda j: (l, 0, 0, 0)),
            wspec(1, POOL_W),
            wspec(1, POOL_W),
            wspec(D_MODEL, D_MODEL),
        ],
        out_specs=row_spec(D_MODEL),
        scratch_shapes=[pltpu.VMEM((r, LRU_W), F32), pltpu.VMEM((r, LRU_W), F32),
                        pltpu.VMEM((nb, LRU_W), F32), pltpu.VMEM((r, LRU_W), BF16)],
        compiler_params=_cparams(("arbitrary",)),
        name="merge",
    )(xt, mods, g1, perm_t, att_l, att_c, hf, guy, u, pin, pin, pin, *gate_params, w_g,
      p_mla, p_lru, p_pool, pool_w, pool_b, pool_scale, w_out)


def _ffn_kernel(x_ref, xl_ref, xr_ref, mod_ref, g2_ref, wv_ref, wgt_ref, cw_ref, cb_ref, wd_ref,
                o_ref, *, nb, n_lat, n_ctx):
    rows = x_ref.shape[0]
    tt = rows // nb
    t0 = pl.program_id(0) * tt

    x = x_ref[...]
    shift = mod_ref[:, 3 * D_MODEL:4 * D_MODEL]
    scale = mod_ref[:, 4 * D_MODEL:5 * D_MODEL]
    gate2 = mod_ref[:, 5 * D_MODEL:6 * D_MODEL]

    keep_l, keep_r = _segment(t0, tt, n_lat, n_lat + n_ctx)
    norm_mod = lambda v: _modulate(_rms(v, g2_ref[...]), shift, scale, nb)
    he = jnp.concatenate([norm_mod(xl_ref[...]) * keep_l, norm_mod(x), norm_mod(xr_ref[...]) * keep_r],
                         axis=0).astype(BF16)
    gate = _dot(he, wgt_ref[...])
    ch = (cw_ref[0:1, :] * gate[0:rows] + cw_ref[1:2, :] * gate[nb:nb + rows]
          + cw_ref[2:3, :] * gate[2 * nb:2 * nb + rows] + cb_ref[...])
    val = _dot(he[nb:nb + rows], wv_ref[...])
    act = ((ch * jnp.tanh(ch) + ch) * val).astype(BF16)
    o_ref[...] = x + _gate_rows(_dot(act, wd_ref[...]), gate2, nb)


def _ffn_call(l, xt, mods, g2, w_val, w_gate, conv_w, conv_b, w_down, *, nb, n_lat, n_ctx, last):
    rows = xt.shape[0]
    r = ROWS_FFN
    tt = r // nb
    lat_tiles = (n_lat * nb) // r
    live_pos = n_lat if last else n_lat + n_ctx
    seg = lambda i: jnp.where(i >= lat_tiles, 1, 0)
    wspec = lambda a, b: _resident((None, a, b), lambda i: (l, 0, 0))
    return pl.pallas_call(
        functools.partial(_ffn_kernel, nb=nb, n_lat=n_lat, n_ctx=n_ctx),
        out_shape=jax.ShapeDtypeStruct((rows, D_MODEL), F32),
        grid=(live_pos * nb // r,),
        in_specs=[
            pl.BlockSpec((r, D_MODEL), lambda i: (i, 0)),
            pl.BlockSpec((nb, D_MODEL), lambda i: (jnp.maximum(i * tt - 1, 0), 0)),
            pl.BlockSpec((nb, D_MODEL), lambda i: (jnp.minimum((i + 1) * tt, live_pos - 1), 0)),
            pl.BlockSpec((None, None, nb, N_MODS * D_MODEL), lambda i: (l, seg(i), 0, 0)),
            wspec(1, D_MODEL),
            wspec(D_MODEL, D_FF),
            wspec(D_MODEL, D_FF),
            wspec(FFN_CONV, D_FF),
            wspec(1, D_FF),
            wspec(D_FF, D_MODEL),
        ],
        out_specs=pl.BlockSpec((r, D_MODEL), lambda i: (i, 0)),
        compiler_params=_cparams(("parallel",)),
        name="ffn",
    )(xt, xt, xt, mods, g2, w_val, w_gate, conv_w, conv_b, w_down)


def _final_kernel(x_ref, g_ref, perm_ref, o_ref):
    nb = o_ref.shape[0]
    ts = ROWS_PERM // nb
    y = _rms(x_ref[...], g_ref[...])
    hi = y.astype(BF16)
    r1 = y - hi.astype(F32)
    mid = r1.astype(BF16)
    lo = (r1 - mid.astype(F32)).astype(BF16)
    p = perm_ref[...]
    for s in range(x_ref.shape[0] // ROWS_PERM):
        sub = slice(s * ROWS_PERM, (s + 1) * ROWS_PERM)
        yp = (_dot(p, hi[sub]) + _dot(p, mid[sub])) + _dot(p, lo[sub])
        for b in range(nb):
            o_ref[b, s * ts:(s + 1) * ts, :] = yp[b * ts:(b + 1) * ts]


def _final_call(xt, g, perm, *, nb, n_lat):
    r = ROWS_OUT
    tt = r // nb
    return pl.pallas_call(
        _final_kernel,
        out_shape=jax.ShapeDtypeStruct((nb, n_lat, D_MODEL), F32),
        grid=(n_lat * nb // r,),
        in_specs=[
            pl.BlockSpec((r, D_MODEL), lambda i: (i, 0)),
            _resident((1, D_MODEL), lambda i: (0, 0)),
            _resident((ROWS_PERM, ROWS_PERM), lambda i: (0, 0)),
        ],
        out_specs=pl.BlockSpec((nb, tt, D_MODEL), lambda i: (0, i, 0)),
        compiler_params=_cparams(("parallel",)),
        name="final_norm",
    )(xt, g.reshape(1, D_MODEL), perm)


def _rope_tables(n_lat, n_ctx):
    half = ROPE_DIM // 2
    inv = ROPE_THETA ** (-jnp.arange(0, half, 2, dtype=F32) / half)
    t = jnp.arange(n_lat)
    ang_r = (t // GRID_W).astype(F32)[:, None] * inv
    ang_c = (t % GRID_W).astype(F32)[:, None] * inv
    cr, sr, cc, sc = jnp.cos(ang_r), jnp.sin(ang_r), jnp.cos(ang_c), jnp.sin(ang_c)
    cos32 = jnp.concatenate([cr, cr, cc, cc], axis=1)
    sin32 = jnp.concatenate([-sr, sr, -sc, sc], axis=1)
    cos32 = jnp.concatenate([cos32, jnp.ones((n_ctx, ROPE_DIM), F32)], axis=0)
    sin32 = jnp.concatenate([sin32, jnp.zeros((n_ctx, ROPE_DIM), F32)], axis=0)
    pad = lambda a: jnp.pad(a, ((0, 0), (NOPE_DIM, HEAD_PAD - NOPE_DIM - ROPE_DIM)))
    return pad(cos32), pad(sin32)


def _rope_partner(w):
    q = ROPE_DIM // 4
    return jnp.concatenate([w[..., q:2 * q], w[..., 0:q], w[..., 3 * q:4 * q], w[..., 2 * q:3 * q]],
                           axis=-1)


def _pack_weights(w_in, w_uq, w_ukv):
    depth = w_in.shape[0]
    zeros = lambda n: jnp.zeros((depth, D_MODEL, n), w_in.dtype)
    w_kr = w_in[:, :, COL_KV:COL_KR]
    tail = HEAD_PAD - NOPE_DIM - ROPE_DIM
    kr_grp = lambda w: jnp.concatenate([zeros(NOPE_DIM), w, zeros(tail)], axis=-1)
    w_att = jnp.concatenate([
        w_in[:, :, :COL_KV], kr_grp(w_kr), kr_grp(_rope_partner(w_kr)),
        w_in[:, :, COL_UX:COL_Q]], axis=-1).astype(BF16)
    w_seq = jnp.concatenate([w_in[:, :, COL_KR:COL_UX], w_in[:, :, COL_Q:COL_UY],
                             w_in[:, :, COL_UY:COL_POOL]], axis=-1).astype(BF16)
    w_g = (0.5 * w_in[:, :, COL_POOL:]).astype(BF16)

    uq = w_uq.reshape(depth, Q_RANK, N_HEADS, NOPE_DIM + ROPE_DIM)
    qn, qr = uq[..., :NOPE_DIM], uq[..., NOPE_DIM:]
    padh = lambda a, lo: jnp.pad(a, ((0, 0), (0, 0), (0, 0), (lo, HEAD_PAD - lo - a.shape[-1])))
    flat = lambda a: a.reshape(depth, a.shape[1], HEADS_W)
    wq2 = jnp.concatenate([flat(padh(qn, 0)), flat(padh(qr, NOPE_DIM)),
                           flat(padh(_rope_partner(qr), NOPE_DIM))], axis=-1).astype(BF16)

    ukv = w_ukv.reshape(depth, KV_RANK, N_HEADS, NOPE_DIM + V_DIM)
    kn, vv = ukv[..., :NOPE_DIM], ukv[..., NOPE_DIM:]
    v_even = padh(vv, 0)
    v_odd = padh(vv, V_DIM)
    odd = (jnp.arange(N_HEADS) % 2 == 1)[None, None, :, None]
    w_kv2 = jnp.concatenate([flat(padh(kn, 0)), flat(jnp.where(odd, v_odd, v_even))],
                            axis=-1).astype(BF16)
    lane = jnp.arange(HEADS_W) % HEAD_PAD
    head_odd = (jnp.arange(HEADS_W) // HEAD_PAD) % 2 == 1
    v_one = jnp.where(head_odd, lane == 0, lane == V_DIM).astype(F32)[None, :]
    return w_att, w_seq, w_g, wq2, w_kv2, v_one


def kernel(x, c, ctx, c_ctx, ada_w, ada_b, norm1_g, norm2_g, w_in, q_norm_g, w_uq, kv_norm_g, w_ukv,
           lru_conv_w, lru_conv_b, lru_wa, lru_ba, lru_wi, lru_bi, lru_lambda, pool_w, pool_b,
           pool_scale, proj_mla, proj_lru, proj_pool, w_out, ffn_up, ffn_conv_w, ffn_conv_b,
           ffn_down, final_norm_g):
    nb, n_lat, _ = x.shape
    n_ctx = ctx.shape[1]
    depth = ada_w.shape[0]
    t_all = n_lat + n_ctx
    assert nb % 8 == 0 and ROWS_PERM % nb == 0
    assert n_lat % Q_TILE == 0 and n_lat % n_ctx == 0
    for r in (ROWS_IN, ROWS_MERGE, ROWS_FFN, ROWS_OUT):
        assert (n_ctx * nb) % r == 0 and (n_lat * nb) % r == 0
        assert r % (POOL_HALO * nb) == 0 and r % ROWS_PERM == 0

    xt = jnp.concatenate([x, ctx], axis=1).transpose(1, 0, 2).reshape(t_all * nb, D_MODEL)

    cond_rows = -(-(nb + 1) // 8) * 8
    cond = jnp.zeros((cond_rows, D_MODEL), F32).at[:nb].set(c).at[nb].set(c_ctx)
    mods = _ada_call(cond, ada_w, ada_b)
    mods = jnp.stack([mods[:, :nb],
                      jnp.broadcast_to(mods[:, nb:nb + 1], (depth, nb, N_MODS * D_MODEL))], axis=1)

    w_att, w_seq, w_g, wq2, w_kv2, v_one = _pack_weights(w_in, w_uq, w_ukv)
    cos_t, sin_t = _rope_tables(n_lat, n_ctx)
    perm = _row_perm(ROWS_PERM, nb)
    perm_t = perm.T
    row3 = lambda a: a.reshape(depth, 1, a.shape[-1])
    vec4 = lambda a: a.reshape(depth, 2, 1, a.shape[-1])
    g1, g2 = row3(norm1_g), row3(norm2_g)
    kv_g, q_g = row3(kv_norm_g), row3(q_norm_g)
    conv_params = (lru_conv_w, row3(lru_conv_b))
    gate_params = ((0.5 * lru_wa).astype(BF16), (0.5 * lru_wi).astype(BF16), vec4(0.5 * lru_ba),
                   vec4(0.5 * lru_bi), vec4(lru_lambda))
    p_mla, p_lru, p_pool = proj_mla.astype(BF16), proj_lru.astype(BF16), proj_pool.astype(BF16)
    pool_wb, w_out_b = pool_w.astype(BF16), w_out.astype(BF16)
    w_val, w_gate = ffn_up[:, :, :D_FF].astype(BF16), ffn_up[:, :, D_FF:].astype(BF16)
    w_down = ffn_down.astype(BF16)

    dims = dict(nb=nb, n_lat=n_lat, n_ctx=n_ctx)
    for l in range(depth):
        last = l == depth - 1
        q, k, v, u, guy, pin, hf = _inproj_call(l, xt, mods, g1, perm, w_att, w_seq, kv_g, w_kv2,
                                                q_g, wq2, cos_t, sin_t, v_one,
                                                conv_params + gate_params, **dims)
        att_l = _attn_call(q, k, v, nb=nb, q_tile=Q_TILE, q_off=0, n_q=n_lat, k_rows=t_all, k_off=0)
        att_c = att_l if last else _attn_call(q, k, v, nb=nb, q_tile=n_ctx, q_off=n_lat, n_q=n_ctx,
                                              k_rows=n_ctx, k_off=n_lat)
        xt = _merge_call(l, xt, mods, g1, perm_t, att_l, att_c, hf, guy, u, pin, gate_params, w_g,
                         p_mla, p_lru, p_pool, pool_wb, row3(pool_b), row3(pool_scale), w_out_b,
                         **dims)
        xt = _ffn_call(l, xt, mods, g2, w_val, w_gate, 0.5 * ffn_conv_w, row3(0.5 * ffn_conv_b), w_down,
                       last=last, **dims)
    return _final_call(xt, final_norm_g, perm, nb=nb, n_lat=n_lat)
```

```python
import functools
import math

import jax
import jax.numpy as jnp
from jax import lax
from jax.experimental import pallas as pl
from jax.experimental.pallas import tpu as pltpu

F32 = jnp.float32
BF16 = jnp.bfloat16

D_MODEL = 1024
GRID_W = 64
EPS = 1e-6
N_MODS = 6

N_HEADS = 8
Q_RANK = 256
KV_RANK = 128
NOPE_DIM = 64
ROPE_DIM = 32
V_DIM = 64
MLA_W = N_HEADS * V_DIM
SM_SCALE = (NOPE_DIM + ROPE_DIM) ** -0.5
ROPE_THETA = 10000.0
LOG2E = math.log2(math.e)

LRU_W = D_MODEL
LRU_BLOCKS = 8
LRU_BS = LRU_W // LRU_BLOCKS
LRU_CONV = 4
LRU_LEFT = LRU_CONV // 2
LRU_RIGHT = LRU_CONV - 1 - LRU_LEFT
LRU_C = 8.0

POOL_WINDOWS = (2, 4, 8, 16)
POOL_W = D_MODEL // 2
POOL_G = POOL_W // len(POOL_WINDOWS)

D_FF = 2816
FFN_CONV = 3

COL_KV = KV_RANK
COL_KR = COL_KV + ROPE_DIM
COL_UX = COL_KR + LRU_W
COL_Q = COL_UX + Q_RANK
COL_UY = COL_Q + LRU_W
COL_POOL = COL_UY + POOL_W

LANE = 128
HEAD_PAD = LANE
HEADS_W = N_HEADS * HEAD_PAD
PA_CKV, PA_KR, PA_KRP, PA_CQ = 0, 128, 256, 384
PA_ATT = PA_CQ + Q_RANK
PA_SEQ = 2 * LRU_W + POOL_W

ROWS_PERM = 256
ROWS_IN = 512
ROWS_MERGE = 512
ROWS_FFN = 512
ROWS_OUT = 512
Q_TILE = 512
POOL_HALO = max(POOL_WINDOWS) // 2
VMEM_LIMIT = 56 * 1024 * 1024


def _cparams(sem):
    return pltpu.CompilerParams(dimension_semantics=sem, vmem_limit_bytes=VMEM_LIMIT)


def _resident(block_shape, index_map):
    return pl.BlockSpec(block_shape, index_map, pipeline_mode=pl.Buffered(1))


def _rms(x, g):
    return x * lax.rsqrt(jnp.mean(x * x, axis=-1, keepdims=True) + EPS) * g


def _modulate(y, shift, scale, nb):
    r, d = y.shape
    y3 = y.reshape(r // nb, nb, d)
    return (y3 * (1.0 + scale)[None] + shift[None]).reshape(r, d)


def _gate_rows(y, gate, nb):
    r, d = y.shape
    return (y.reshape(r // nb, nb, d) * gate[None]).reshape(r, d)


def _dot(a, b):
    return jnp.dot(a, b, preferred_element_type=F32)


def _row_perm(rows, nb):
    tt = rows // nb
    dst = jnp.arange(rows)
    src = (dst % tt) * nb + dst // tt
    return (src[:, None] == jnp.arange(rows)[None, :]).astype(BF16)


def _tile_of(j, lat_tiles, n_tiles, reverse):
    ctx_tiles = n_tiles - lat_tiles
    if reverse:
        return jnp.where(j < ctx_tiles, n_tiles - 1 - j, lat_tiles - 1 - (j - ctx_tiles))
    return jnp.where(j < ctx_tiles, lat_tiles + j, j - ctx_tiles)


def _segment(t0, tt, n_lat, n_all):
    seg_start = jnp.logical_or(t0 == 0, t0 == n_lat)
    seg_end = jnp.logical_or(t0 + tt == n_lat, t0 + tt == n_all)
    return jnp.where(seg_start, 0.0, 1.0), jnp.where(seg_end, 0.0, 1.0)


def _regroup_rows(perm, y):
    hi = y.astype(BF16)
    r1 = y - hi.astype(F32)
    mid = r1.astype(BF16)
    lo = (r1 - mid.astype(F32)).astype(BF16)
    return (_dot(perm, hi) + _dot(perm, mid)) + _dot(perm, lo)


def _to_time_major_kernel(x_ref, c_ref, perm_t_ref, o_ref, *, lat_tiles):
    nb = x_ref.shape[0]
    ts = ROWS_PERM // nb

    def emit(src_ref):
        for s in range(o_ref.shape[0] // ROWS_PERM):
            rows_bt = jnp.concatenate([src_ref[b, s * ts:(s + 1) * ts, :] for b in range(nb)], axis=0)
            o_ref[s * ROWS_PERM:(s + 1) * ROWS_PERM, :] = _regroup_rows(perm_t_ref[...], rows_bt)

    @pl.when(pl.program_id(0) < lat_tiles)
    def _():
        emit(x_ref)

    @pl.when(pl.program_id(0) >= lat_tiles)
    def _():
        emit(c_ref)


def _to_time_major_call(x, ctx, perm_t):
    nb, n_lat, _ = x.shape
    n_ctx = ctx.shape[1]
    r = ROWS_OUT
    tt = r // nb
    lat_tiles = n_lat // tt
    return pl.pallas_call(
        functools.partial(_to_time_major_kernel, lat_tiles=lat_tiles),
        out_shape=jax.ShapeDtypeStruct(((n_lat + n_ctx) * nb, D_MODEL), F32),
        grid=((n_lat + n_ctx) // tt,),
        in_specs=[
            pl.BlockSpec((nb, tt, D_MODEL), lambda i: (0, jnp.minimum(i, lat_tiles - 1), 0)),
            pl.BlockSpec((nb, tt, D_MODEL), lambda i: (0, jnp.maximum(i - lat_tiles, 0), 0)),
            _resident((ROWS_PERM, ROWS_PERM), lambda i: (0, 0)),
        ],
        out_specs=pl.BlockSpec((r, D_MODEL), lambda i: (i, 0)),
        compiler_params=_cparams(("parallel",)),
        name="to_time_major",
    )(x, ctx, perm_t)


def _ada_kernel(cond_ref, w_ref, b_ref, o_ref):
    c = cond_ref[...]
    s = (c * jax.nn.sigmoid(c)).astype(BF16)
    o_ref[...] = _dot(s, w_ref[...].astype(BF16)) + b_ref[...]


def _ada_call(cond, ada_w, ada_b):
    depth = ada_w.shape[0]
    rows = cond.shape[0]
    n_out = ada_w.shape[2]
    tn = n_out // 4
    return pl.pallas_call(
        _ada_kernel,
        out_shape=jax.ShapeDtypeStruct((depth, rows, n_out), F32),
        grid=(depth, n_out // tn),
        in_specs=[
            pl.BlockSpec((rows, D_MODEL), lambda l, j: (0, 0)),
            pl.BlockSpec((None, D_MODEL, tn), lambda l, j: (l, 0, j)),
            pl.BlockSpec((None, 1, tn), lambda l, j: (l, 0, j)),
        ],
        out_specs=pl.BlockSpec((None, rows, tn), lambda l, j: (l, 0, j)),
        compiler_params=_cparams(("parallel", "parallel")),
        name="ada",
    )(cond, ada_w, ada_b.reshape(depth, 1, n_out))


def _lru_coeffs(u_of, wa_ref, wi_ref, ba_ref, bi_ref, lam_ref, a_s, b_s, fillers=()):
    fillers = list(fillers)
    for blk in range(LRU_BLOCKS):
        if blk < len(fillers):
            fillers[blk]()
        sl = slice(blk * LRU_BS, (blk + 1) * LRU_BS)
        u = u_of(sl)
        ub = u.astype(BF16)
        ta = jnp.tanh(_dot(ub, wa_ref[blk]) + ba_ref[:, sl])
        ti = jnp.tanh(_dot(ub, wi_ref[blk]) + bi_ref[:, sl])
        z = -lam_ref[:, sl]
        softplus = jnp.maximum(z, 0.0) + jnp.log(1.0 + jnp.exp(-jnp.abs(z)))
        ch = (-0.5 * LRU_C * LOG2E) * softplus
        a = jnp.exp2(ta * ch + ch)
        y = 1.0 - a * a
        root = jnp.where(y > 0.0, y * lax.rsqrt(y), 0.0)
        uh = 0.5 * u
        a_s[:, sl] = a
        b_s[:, sl] = (ti * uh + uh) * root


def _lru_gate_specs(l, d):
    vec = _resident((None, None, 1, LRU_W), lambda j: (l, d, 0, 0))
    mat = _resident((None, None, LRU_BLOCKS, LRU_BS, LRU_BS), lambda j: (l, d, 0, 0, 0))
    return [mat, mat, vec, vec, vec]


def _inproj_kernel(x_ref, xr_ref, mod_ref, g1_ref, perm_ref, watt_ref, wseq_ref, kvg_ref, wkv_ref,
                   qg_ref, wq_ref, cos_ref, sin_ref, vone_ref, cw_ref, cb_ref, wa_ref, wi_ref,
                   ba_ref, bi_ref, lam_ref, q_ref, k_ref, v_ref, u_ref, guy_ref, pin_ref, hf_ref,
                   a_s, b_s, h_s, tail_s, *, nb, n_lat, n_ctx):
    j = pl.program_id(0)
    rows = x_ref.shape[0]
    tt = rows // nb
    n_all = n_lat + n_ctx
    tile = _tile_of(j, n_lat // tt, n_all // tt, False)
    keep_l, keep_r = _segment(tile * tt, tt, n_lat, n_all)
    n_sub = rows // ROWS_PERM
    ts = ROWS_PERM // nb
    shift = mod_ref[:, 0:D_MODEL]
    scale = mod_ref[:, D_MODEL:2 * D_MODEL]
    norm_mod = lambda v: _modulate(_rms(v, g1_ref[...]), shift, scale, nb)

    @pl.when(j == 0)
    def _():
        h_s[...] = jnp.zeros_like(h_s)
        tail_s[...] = jnp.zeros_like(tail_s)

    h = norm_mod(x_ref[...]).astype(BF16)

    h_next = (norm_mod(xr_ref[...]) * keep_r).astype(BF16)
    ux_ext = _dot(jnp.concatenate([h, h_next], axis=0), wseq_ref[:, 0:LRU_W])

    ext = jnp.concatenate([tail_s[...] * keep_l, ux_ext], axis=0)
    tail_s[...] = ux_ext[rows - LRU_LEFT * nb:rows]

    def conv(sl):
        u = cb_ref[:, sl]
        for tap in range(LRU_CONV):
            u = u + cw_ref[tap:tap + 1, sl] * ext[tap * nb:tap * nb + rows, sl]
        u_ref[:, sl] = u
        return u

    tile_pos = lambda ref: jnp.concatenate(
        [ref[s * ts:(s + 1) * ts] for s in range(n_sub) for _ in range(nb)], axis=0)
    val = {}

    def to_positions(ref, arr):
        for s in range(n_sub):
            for b in range(nb):
                ref[s * ts:(s + 1) * ts, b * HEADS_W:(b + 1) * HEADS_W] = (
                    arr[s * ROWS_PERM + b * ts:s * ROWS_PERM + (b + 1) * ts])

    def f_guy():
        guy_ref[...] = jax.nn.gelu(_dot(h, wseq_ref[:, LRU_W:2 * LRU_W])).astype(guy_ref.dtype)

    def f_pin():
        pin_ref[...] = _dot(h, wseq_ref[:, 2 * LRU_W:])

    def f_att():
        perm = perm_ref[...]
        hp = jnp.concatenate([_dot(perm, h[s * ROWS_PERM:(s + 1) * ROWS_PERM]).astype(BF16)
                              for s in range(n_sub)], axis=0)
        val["p"] = _dot(hp, watt_ref[...])
        val["cos"] = tile_pos(cos_ref)
        val["sin"] = tile_pos(sin_ref)

    def f_kv():
        ckvn = _rms(val["p"][:, PA_CKV:PA_CKV + KV_RANK], kvg_ref[...]).astype(BF16)
        val["kv"] = _dot(ckvn, wkv_ref[...])

    def f_k():
        p = val["p"]
        kr = p[:, PA_KR:PA_KR + LANE] * val["cos"] + p[:, PA_KRP:PA_KRP + LANE] * val["sin"]
        k = val["kv"][:, :HEADS_W] + jnp.concatenate([kr] * N_HEADS, axis=1)
        to_positions(k_ref, k.astype(k_ref.dtype))

    def f_v():
        to_positions(v_ref, (val["kv"][:, HEADS_W:] + vone_ref[...]).astype(v_ref.dtype))

    def f_qq():
        cqn = _rms(val["p"][:, PA_CQ:PA_CQ + Q_RANK], qg_ref[...]).astype(BF16)
        val["qq"] = _dot(cqn, wq_ref[...])

    def f_q():
        qq = val["qq"]
        cos_h = jnp.concatenate([val["cos"]] * N_HEADS, axis=1)
        sin_h = jnp.concatenate([val["sin"]] * N_HEADS, axis=1)
        q = (qq[:, :HEADS_W] + qq[:, HEADS_W:2 * HEADS_W] * cos_h + qq[:, 2 * HEADS_W:] * sin_h
             ) * (SM_SCALE * LOG2E)
        to_positions(q_ref, q.astype(q_ref.dtype))

    _lru_coeffs(conv, wa_ref, wi_ref, ba_ref, bi_ref, lam_ref, a_s, b_s,
                fillers=(f_guy, f_att, f_pin, f_kv, f_k, f_qq, f_v, f_q))

    def step(t, hc):
        rs = pl.ds(pl.multiple_of(t * nb, nb), nb)
        hc = a_s[rs, :] * hc + b_s[rs, :]
        hf_ref[rs, :] = hc.astype(hf_ref.dtype)
        return hc

    h_s[...] = lax.fori_loop(0, tt, step, h_s[...], unroll=8)


def _inproj_call(l, xt, mods, g1, perm, w_att, w_seq, kv_g, w_kv2, q_g, wq2, cos_t, sin_t, v_one,
                 lru_params, *, nb, n_lat, n_ctx):
    rows = xt.shape[0]
    t_all = rows // nb
    r = ROWS_IN
    tt = r // nb
    n_tiles = rows // r
    lat_tiles = (n_lat * nb) // r
    tile = lambda j: _tile_of(j, lat_tiles, n_tiles, False)
    seg = lambda j: jnp.where(tile(j) >= lat_tiles, 1, 0)
    row_spec = lambda w: pl.BlockSpec((r, w), lambda j: (tile(j), 0))
    pos_spec = lambda w: pl.BlockSpec((tt, w), lambda j: (tile(j), 0))
    wspec = lambda a, b: _resident((None, a, b), lambda j: (l, 0, 0))
    return pl.pallas_call(
        functools.partial(_inproj_kernel, nb=nb, n_lat=n_lat, n_ctx=n_ctx),
        out_shape=(
            jax.ShapeDtypeStruct((t_all, nb * HEADS_W), BF16),
            jax.ShapeDtypeStruct((t_all, nb * HEADS_W), BF16),
            jax.ShapeDtypeStruct((t_all, nb * HEADS_W), BF16),
            jax.ShapeDtypeStruct((rows, LRU_W), F32),
            jax.ShapeDtypeStruct((rows, LRU_W), BF16),
            jax.ShapeDtypeStruct((rows, POOL_W), F32),
            jax.ShapeDtypeStruct((rows, LRU_W), BF16),
        ),
        grid=(n_tiles,),
        in_specs=[
            row_spec(D_MODEL),
            pl.BlockSpec((nb, D_MODEL), lambda j: (jnp.minimum((tile(j) + 1) * tt, t_all - 1), 0)),
            pl.BlockSpec((None, None, nb, N_MODS * D_MODEL), lambda j: (l, seg(j), 0, 0)),
            wspec(1, D_MODEL),
            _resident((ROWS_PERM, ROWS_PERM), lambda j: (0, 0)),
            wspec(D_MODEL, PA_ATT),
            wspec(D_MODEL, PA_SEQ),
            wspec(1, KV_RANK),
            wspec(KV_RANK, 2 * HEADS_W),
            wspec(1, Q_RANK),
            wspec(Q_RANK, 3 * HEADS_W),
            pos_spec(LANE),
            pos_spec(LANE),
            _resident((1, HEADS_W), lambda j: (0, 0)),
            _resident((None, LRU_CONV, LRU_W), lambda j: (l, 0, 0)),
            _resident((None, 1, LRU_W), lambda j: (l, 0, 0)),
        ] + _lru_gate_specs(l, 0),
        out_specs=(pos_spec(nb * HEADS_W), pos_spec(nb * HEADS_W), pos_spec(nb * HEADS_W),
                   row_spec(LRU_W), row_spec(LRU_W), row_spec(POOL_W), row_spec(LRU_W)),
        scratch_shapes=[pltpu.VMEM((r, LRU_W), F32), pltpu.VMEM((r, LRU_W), F32),
                        pltpu.VMEM((nb, LRU_W), F32), pltpu.VMEM((LRU_LEFT * nb, LRU_W), F32)],
        compiler_params=_cparams(("arbitrary",)),
        name="inproj",
    )(xt, xt, mods, g1, perm, w_att, w_seq, kv_g, w_kv2, q_g, wq2, cos_t, sin_t, v_one, *lru_params)


def _attn_kernel(q_ref, k_ref, v_ref, o_ref):
    tq = q_ref.shape[0]
    lane = lax.broadcasted_iota(jnp.int32, (tq, LANE), 1)
    for pair in range(N_HEADS // 2):
        res = []
        for hh in range(2):
            hs = slice((2 * pair + hh) * HEAD_PAD, (2 * pair + hh + 1) * HEAD_PAD)
            s = lax.dot_general(q_ref[:, hs], k_ref[:, hs], (((1,), (1,)), ((), ())),
                                preferred_element_type=F32)
            e = jnp.exp2(s - jnp.max(s, axis=-1, keepdims=True))
            res.append(_dot(e.astype(BF16), v_ref[:, hs]))
        den0 = res[0][:, V_DIM:V_DIM + 1]
        den1 = res[1][:, 0:1]
        o_ref[:, pair * LANE:(pair + 1) * LANE] = jnp.where(
            lane < V_DIM, res[0] * (1.0 / den0), res[1] * (1.0 / den1)).astype(o_ref.dtype)


def _attn_call(q, k, v, *, nb, q_tile, q_off, n_q, k_rows, k_off):
    qo, ko = q_off // q_tile, k_off // k_rows
    return pl.pallas_call(
        _attn_kernel,
        out_shape=jax.ShapeDtypeStruct((n_q, nb * MLA_W), BF16),
        grid=(nb, n_q // q_tile),
        in_specs=[
            pl.BlockSpec((q_tile, HEADS_W), lambda b, i: (i + qo, b)),
            pl.BlockSpec((k_rows, HEADS_W), lambda b, i: (ko, b)),
            pl.BlockSpec((k_rows, HEADS_W), lambda b, i: (ko, b)),
        ],
        out_specs=pl.BlockSpec((q_tile, MLA_W), lambda b, i: (i, b)),
        compiler_params=_cparams(("parallel", "arbitrary")),
        name="attn",
    )(q, k, v)


def _pool_diff(ext, t0, nb, rows, n_lat, n_ctx):
    halo = POOL_HALO * nb

    def shifted_sum(arr, start, k):
        n = arr.shape[0]
        return arr[2 * k:n] + arr[0:n - 2 * k], start + k

    cur, cst = ext[nb:] + ext[:-nb], nb
    sums = {POOL_WINDOWS[0]: (cur, cst)}
    step = nb
    for w in POOL_WINDOWS[1:]:
        cur, cst = shifted_sum(cur, cst, step)
        sums[w] = (cur, cst)
        step *= 2

    t = t0 + lax.broadcasted_iota(jnp.int32, (rows // nb, nb, POOL_G), 0).reshape(rows, POOL_G)
    in_lat = t < n_lat
    t_loc = jnp.where(in_lat, t, t - n_lat)
    n_seg = jnp.where(in_lat, n_lat, n_ctx)
    outs = []
    for g, w in enumerate(POOL_WINDOWS):
        arr, start = sums[w]
        lanes = slice(g * POOL_G, (g + 1) * POOL_G)
        tot = arr[halo - start:halo - start + rows, lanes]
        left = w // 2
        right = w - 1 - left
        cnt = (jnp.minimum(t_loc + right + 1, n_seg) - jnp.maximum(t_loc - left, 0)).astype(F32)
        outs.append(tot / cnt - ext[halo:halo + rows, lanes])
    return outs


def _merge_kernel(x_ref, mod_ref, g1_ref, perm_ref, attl_ref, attc_ref, hf_ref, guy_ref, u_ref,
                  pin_ref, pl_ref, pr_ref, wa_ref, wi_ref, ba_ref, bi_ref, lam_ref, wg_ref, pm_ref,
                  plru_ref, pp_ref, pw_ref, pb_ref, ps_ref, wo_ref, o_ref, a_s, b_s, h_s, rec_s,
                  *, nb, n_lat, n_ctx):
    j = pl.program_id(0)
    rows = x_ref.shape[0]
    tt = rows // nb
    n_all = n_lat + n_ctx
    lat_tiles = n_lat // tt
    tile = _tile_of(j, lat_tiles, n_all // tt, True)
    t0 = tile * tt
    keep_l, keep_r = _segment(t0, tt, n_lat, n_all)

    @pl.when(j == 0)
    def _():
        h_s[...] = jnp.zeros_like(h_s)

    x = x_ref[...]
    shift = mod_ref[:, 0:D_MODEL]
    scale = mod_ref[:, D_MODEL:2 * D_MODEL]
    gate1 = mod_ref[:, 2 * D_MODEL:3 * D_MODEL]
    h = _modulate(_rms(x, g1_ref[...]), shift, scale, nb).astype(BF16)
    val = {}

    def gate(kk):
        return 0.5 * jnp.tanh(_dot(h, wg_ref[:, kk * D_MODEL:(kk + 1) * D_MODEL])) + 0.5

    def f_pool():
        pext = jnp.concatenate([pl_ref[...] * keep_l, pin_ref[...], pr_ref[...] * keep_r], axis=0)
        diffs = _pool_diff(pext, t0, nb, rows, n_lat, n_ctx)
        pooled = [_dot(dg.astype(BF16), pw_ref[g]) for g, dg in enumerate(diffs)]
        val["pool"] = ((jnp.concatenate(pooled, axis=1) + pb_ref[...]) * ps_ref[...]).astype(BF16)

    def f_att():
        is_ctx = jnp.where(tile >= lat_tiles, 1.0, 0.0)
        att_pos = (attl_ref[...].astype(F32) * (1.0 - is_ctx)
                   + attc_ref[...].astype(F32) * is_ctx).astype(BF16)
        ts = ROWS_PERM // nb
        perm_t = perm_ref[...]
        val["att"] = jnp.concatenate([
            _dot(perm_t, jnp.concatenate([att_pos[s * ts:(s + 1) * ts, b * MLA_W:(b + 1) * MLA_W]
                                          for b in range(nb)], axis=0)).astype(BF16)
            for s in range(rows // ROWS_PERM)], axis=0)

    def f_m_att():
        val["m"] = gate(0) * _dot(val["att"], pm_ref[...])

    def f_m_pool():
        val["m"] = val["m"] + gate(2) * _dot(val["pool"], pp_ref[...])

    def f_g_rec():
        val["g_rec"] = gate(1)

    _lru_coeffs(lambda sl: u_ref[:, sl], wa_ref, wi_ref, ba_ref, bi_ref, lam_ref, a_s, b_s,
                fillers=(f_att, f_pool, f_m_att, f_m_pool, f_g_rec))
    m, g_rec = val["m"], val["g_rec"]

    def step(s, hc):
        rs = pl.ds(pl.multiple_of((tt - 1 - s) * nb, nb), nb)
        hc = a_s[rs, :] * hc + b_s[rs, :]
        rec_s[rs, :] = ((hf_ref[rs, :].astype(F32) + hc)
                        * guy_ref[rs, :].astype(F32)).astype(rec_s.dtype)
        return hc

    h_s[...] = lax.fori_loop(0, tt, step, h_s[...], unroll=8)

    m = m + g_rec * _dot(rec_s[...], plru_ref[...])
    y = _dot(m.astype(BF16), wo_ref[...])
    o_ref[...] = x + _gate_rows(y, gate1, nb)


def _merge_call(l, xt, mods, g1, perm_t, att_l, att_c, hf, guy, u, pin, gate_params, w_g, p_mla,
                p_lru, p_pool, pool_w, pool_b, pool_scale, w_out, *, nb, n_lat, n_ctx):
    rows = xt.shape[0]
    r = ROWS_MERGE
    tt = r // nb
    hb = POOL_HALO * nb
    n_tiles = rows // r
    lat_tiles = (n_lat * nb) // r
    tile = lambda j: _tile_of(j, lat_tiles, n_tiles, True)
    seg = lambda j: jnp.where(tile(j) >= lat_tiles, 1, 0)
    row_spec = lambda w: pl.BlockSpec((r, w), lambda j: (tile(j), 0))
    wspec = lambda a, b: _resident((None, a, b), lambda j: (l, 0, 0))
    return pl.pallas_call(
        functools.partial(_merge_kernel, nb=nb, n_lat=n_lat, n_ctx=n_ctx),
        out_shape=jax.ShapeDtypeStruct((rows, D_MODEL), F32),
        grid=(n_tiles,),
        in_specs=[
            row_spec(D_MODEL),
            pl.BlockSpec((None, None, nb, N_MODS * D_MODEL), lambda j: (l, seg(j), 0, 0)),
            wspec(1, D_MODEL),
            _resident((ROWS_PERM, ROWS_PERM), lambda j: (0, 0)),
            pl.BlockSpec((tt, nb * MLA_W), lambda j: (jnp.minimum(tile(j), lat_tiles - 1), 0)),
            pl.BlockSpec((tt, nb * MLA_W), lambda j: (jnp.maximum(tile(j) - lat_tiles, 0), 0)),
            row_spec(LRU_W),
            row_spec(LRU_W),
            row_spec(LRU_W),
            row_spec(POOL_W),
            pl.BlockSpec((hb, POOL_W), lambda j: (jnp.maximum(tile(j) * (r // hb) - 1, 0), 0)),
            pl.BlockSpec((hb, POOL_W),
                         lambda j: (jnp.minimum((tile(j) + 1) * (r // hb), rows // hb - 1), 0)),
        ] + _lru_gate_specs(l, 1) + [
            wspec(D_MODEL, 3 * D_MODEL),
            wspec(MLA_W, D_MODEL),
            wspec(LRU_W, D_MODEL),
            wspec(POOL_W, D_MODEL),
            _resident((None, len(POOL_WINDOWS), POOL_G, POOL_G), lambda j: (l, 0, 0, 0)),
            wspec(1, POOL_W),
            wspec(1, POOL_W),
            wspec(D_MODEL, D_MODEL),
        ],
        out_specs=row_spec(D_MODEL),
        scratch_shapes=[pltpu.VMEM((r, LRU_W), F32), pltpu.VMEM((r, LRU_W), F32),
                        pltpu.VMEM((nb, LRU_W), F32), pltpu.VMEM((r, LRU_W), BF16)],
        compiler_params=_cparams(("arbitrary",)),
        name="merge",
    )(xt, mods, g1, perm_t, att_l, att_c, hf, guy, u, pin, pin, pin, *gate_params, w_g,
      p_mla, p_lru, p_pool, pool_w, pool_b, pool_scale, w_out)


def _ffn_kernel(x_ref, xl_ref, xr_ref, mod_ref, g2_ref, wv_ref, wgt_ref, cw_ref, cb_ref, wd_ref,
                o_ref, *, nb, n_lat, n_ctx):
    rows = x_ref.shape[0]
    tt = rows // nb
    t0 = pl.program_id(0) * tt

    x = x_ref[...]
    shift = mod_ref[:, 3 * D_MODEL:4 * D_MODEL]
    scale = mod_ref[:, 4 * D_MODEL:5 * D_MODEL]
    gate2 = mod_ref[:, 5 * D_MODEL:6 * D_MODEL]

    keep_l, keep_r = _segment(t0, tt, n_lat, n_lat + n_ctx)
    norm_mod = lambda v: _modulate(_rms(v, g2_ref[...]), shift, scale, nb)
    he = jnp.concatenate([norm_mod(xl_ref[...]) * keep_l, norm_mod(x), norm_mod(xr_ref[...]) * keep_r],
                         axis=0).astype(BF16)
    gate = _dot(he, wgt_ref[...])
    ch = (cw_ref[0:1, :] * gate[0:rows] + cw_ref[1:2, :] * gate[nb:nb + rows]
          + cw_ref[2:3, :] * gate[2 * nb:2 * nb + rows] + cb_ref[...])
    val = _dot(he[nb:nb + rows], wv_ref[...])
    act = ((ch * jnp.tanh(ch) + ch) * val).astype(BF16)
    o_ref[...] = x + _gate_rows(_dot(act, wd_ref[...]), gate2, nb)


def _ffn_call(l, xt, mods, g2, w_val, w_gate, conv_w, conv_b, w_down, *, nb, n_lat, n_ctx, last):
    rows = xt.shape[0]
    r = ROWS_FFN
    tt = r // nb
    lat_tiles = (n_lat * nb) // r
    live_pos = n_lat if last else n_lat + n_ctx
    seg = lambda i: jnp.where(i >= lat_tiles, 1, 0)
    wspec = lambda a, b: _resident((None, a, b), lambda i: (l, 0, 0))
    return pl.pallas_call(
        functools.partial(_ffn_kernel, nb=nb, n_lat=n_lat, n_ctx=n_ctx),
        out_shape=jax.ShapeDtypeStruct((rows, D_MODEL), F32),
        grid=(live_pos * nb // r,),
        in_specs=[
            pl.BlockSpec((r, D_MODEL), lambda i: (i, 0)),
            pl.BlockSpec((nb, D_MODEL), lambda i: (jnp.maximum(i * tt - 1, 0), 0)),
            pl.BlockSpec((nb, D_MODEL), lambda i: (jnp.minimum((i + 1) * tt, live_pos - 1), 0)),
            pl.BlockSpec((None, None, nb, N_MODS * D_MODEL), lambda i: (l, seg(i), 0, 0)),
            wspec(1, D_MODEL),
            wspec(D_MODEL, D_FF),
            wspec(D_MODEL, D_FF),
            wspec(FFN_CONV, D_FF),
            wspec(1, D_FF),
            wspec(D_FF, D_MODEL),
        ],
        out_specs=pl.BlockSpec((r, D_MODEL), lambda i: (i, 0)),
        compiler_params=_cparams(("parallel",)),
        name="ffn",
    )(xt, xt, xt, mods, g2, w_val, w_gate, conv_w, conv_b, w_down)


def _final_kernel(x_ref, g_ref, perm_ref, o_ref):
    nb = o_ref.shape[0]
    ts = ROWS_PERM // nb
    y = _rms(x_ref[...], g_ref[...])
    for s in range(x_ref.shape[0] // ROWS_PERM):
        yp = _regroup_rows(perm_ref[...], y[s * ROWS_PERM:(s + 1) * ROWS_PERM])
        for b in range(nb):
            o_ref[b, s * ts:(s + 1) * ts, :] = yp[b * ts:(b + 1) * ts]


def _final_call(xt, g, perm, *, nb, n_lat):
    r = ROWS_OUT
    tt = r // nb
    return pl.pallas_call(
        _final_kernel,
        out_shape=jax.ShapeDtypeStruct((nb, n_lat, D_MODEL), F32),
        grid=(n_lat * nb // r,),
        in_specs=[
            pl.BlockSpec((r, D_MODEL), lambda i: (i, 0)),
            _resident((1, D_MODEL), lambda i: (0, 0)),
            _resident((ROWS_PERM, ROWS_PERM), lambda i: (0, 0)),
        ],
        out_specs=pl.BlockSpec((nb, tt, D_MODEL), lambda i: (0, i, 0)),
        compiler_params=_cparams(("parallel",)),
        name="final_norm",
    )(xt, g.reshape(1, D_MODEL), perm)


def _rope_tables(n_lat, n_ctx):
    half = ROPE_DIM // 2
    inv = ROPE_THETA ** (-jnp.arange(0, half, 2, dtype=F32) / half)
    t = jnp.arange(n_lat)
    ang_r = (t // GRID_W).astype(F32)[:, None] * inv
    ang_c = (t % GRID_W).astype(F32)[:, None] * inv
    cr, sr, cc, sc = jnp.cos(ang_r), jnp.sin(ang_r), jnp.cos(ang_c), jnp.sin(ang_c)
    cos32 = jnp.concatenate([cr, cr, cc, cc], axis=1)
    sin32 = jnp.concatenate([-sr, sr, -sc, sc], axis=1)
    cos32 = jnp.concatenate([cos32, jnp.ones((n_ctx, ROPE_DIM), F32)], axis=0)
    sin32 = jnp.concatenate([sin32, jnp.zeros((n_ctx, ROPE_DIM), F32)], axis=0)
    pad = lambda a: jnp.pad(a, ((0, 0), (NOPE_DIM, HEAD_PAD - NOPE_DIM - ROPE_DIM)))
    return pad(cos32), pad(sin32)


def _rope_partner(w):
    q = ROPE_DIM // 4
    return jnp.concatenate([w[..., q:2 * q], w[..., 0:q], w[..., 3 * q:4 * q], w[..., 2 * q:3 * q]],
                           axis=-1)


def _pack_weights(w_in, w_uq, w_ukv):
    depth = w_in.shape[0]
    zeros = lambda n: jnp.zeros((depth, D_MODEL, n), w_in.dtype)
    w_kr = w_in[:, :, COL_KV:COL_KR]
    tail = HEAD_PAD - NOPE_DIM - ROPE_DIM
    kr_grp = lambda w: jnp.concatenate([zeros(NOPE_DIM), w, zeros(tail)], axis=-1)
    w_att = jnp.concatenate([
        w_in[:, :, :COL_KV], kr_grp(w_kr), kr_grp(_rope_partner(w_kr)),
        w_in[:, :, COL_UX:COL_Q]], axis=-1).astype(BF16)
    w_seq = jnp.concatenate([w_in[:, :, COL_KR:COL_UX], w_in[:, :, COL_Q:COL_UY],
                             w_in[:, :, COL_UY:COL_POOL]], axis=-1).astype(BF16)
    w_g = (0.5 * w_in[:, :, COL_POOL:]).astype(BF16)

    uq = w_uq.reshape(depth, Q_RANK, N_HEADS, NOPE_DIM + ROPE_DIM)
    qn, qr = uq[..., :NOPE_DIM], uq[..., NOPE_DIM:]
    padh = lambda a, lo: jnp.pad(a, ((0, 0), (0, 0), (0, 0), (lo, HEAD_PAD - lo - a.shape[-1])))
    flat = lambda a: a.reshape(depth, a.shape[1], HEADS_W)
    wq2 = jnp.concatenate([flat(padh(qn, 0)), flat(padh(qr, NOPE_DIM)),
                           flat(padh(_rope_partner(qr), NOPE_DIM))], axis=-1).astype(BF16)

    ukv = w_ukv.reshape(depth, KV_RANK, N_HEADS, NOPE_DIM + V_DIM)
    kn, vv = ukv[..., :NOPE_DIM], ukv[..., NOPE_DIM:]
    v_even = padh(vv, 0)
    v_odd = padh(vv, V_DIM)
    odd = (jnp.arange(N_HEADS) % 2 == 1)[None, None, :, None]
    w_kv2 = jnp.concatenate([flat(padh(kn, 0)), flat(jnp.where(odd, v_odd, v_even))],
                            axis=-1).astype(BF16)
    lane = jnp.arange(HEADS_W) % HEAD_PAD
    head_odd = (jnp.arange(HEADS_W) // HEAD_PAD) % 2 == 1
    v_one = jnp.where(head_odd, lane == 0, lane == V_DIM).astype(F32)[None, :]
    return w_att, w_seq, w_g, wq2, w_kv2, v_one


def kernel(x, c, ctx, c_ctx, ada_w, ada_b, norm1_g, norm2_g, w_in, q_norm_g, w_uq, kv_norm_g, w_ukv,
           lru_conv_w, lru_conv_b, lru_wa, lru_ba, lru_wi, lru_bi, lru_lambda, pool_w, pool_b,
           pool_scale, proj_mla, proj_lru, proj_pool, w_out, ffn_up, ffn_conv_w, ffn_conv_b,
           ffn_down, final_norm_g):
    nb, n_lat, _ = x.shape
    n_ctx = ctx.shape[1]
    depth = ada_w.shape[0]
    t_all = n_lat + n_ctx
    assert nb % 8 == 0 and ROWS_PERM % nb == 0
    assert n_lat % Q_TILE == 0 and n_lat % n_ctx == 0
    for r in (ROWS_IN, ROWS_MERGE, ROWS_FFN, ROWS_OUT):
        assert (n_ctx * nb) % r == 0 and (n_lat * nb) % r == 0
        assert r % (POOL_HALO * nb) == 0 and r % ROWS_PERM == 0

    perm = _row_perm(ROWS_PERM, nb)
    perm_t = perm.T
    xt = _to_time_major_call(x, ctx, perm_t)

    cond_rows = -(-(nb + 1) // 8) * 8
    cond = jnp.zeros((cond_rows, D_MODEL), F32).at[:nb].set(c).at[nb].set(c_ctx)
    mods = _ada_call(cond, ada_w, ada_b)
    mods = jnp.stack([mods[:, :nb],
                      jnp.broadcast_to(mods[:, nb:nb + 1], (depth, nb, N_MODS * D_MODEL))], axis=1)

    w_att, w_seq, w_g, wq2, w_kv2, v_one = _pack_weights(w_in, w_uq, w_ukv)
    cos_t, sin_t = _rope_tables(n_lat, n_ctx)
    row3 = lambda a: a.reshape(depth, 1, a.shape[-1])
    vec4 = lambda a: a.reshape(depth, 2, 1, a.shape[-1])
    g1, g2 = row3(norm1_g), row3(norm2_g)
    kv_g, q_g = row3(kv_norm_g), row3(q_norm_g)
    conv_params = (lru_conv_w, row3(lru_conv_b))
    gate_params = ((0.5 * lru_wa).astype(BF16), (0.5 * lru_wi).astype(BF16), vec4(0.5 * lru_ba),
                   vec4(0.5 * lru_bi), vec4(lru_lambda))
    p_mla, p_lru, p_pool = proj_mla.astype(BF16), proj_lru.astype(BF16), proj_pool.astype(BF16)
    pool_wb, w_out_b = pool_w.astype(BF16), w_out.astype(BF16)
    w_val, w_gate = ffn_up[:, :, :D_FF].astype(BF16), ffn_up[:, :, D_FF:].astype(BF16)
    w_down = ffn_down.astype(BF16)

    dims = dict(nb=nb, n_lat=n_lat, n_ctx=n_ctx)
    for l in range(depth):
        last = l == depth - 1
        q, k, v, u, guy, pin, hf = _inproj_call(l, xt, mods, g1, perm, w_att, w_seq, kv_g, w_kv2,
                                                q_g, wq2, cos_t, sin_t, v_one,
                                                conv_params + gate_params, **dims)
        att_l = _attn_call(q, k, v, nb=nb, q_tile=Q_TILE, q_off=0, n_q=n_lat, k_rows=t_all, k_off=0)
        att_c = att_l if last else _attn_call(q, k, v, nb=nb, q_tile=n_ctx, q_off=n_lat, n_q=n_ctx,
                                              k_rows=n_ctx, k_off=n_lat)
        xt = _merge_call(l, xt, mods, g1, perm_t, att_l, att_c, hf, guy, u, pin, gate_params, w_g,
                         p_mla, p_lru, p_pool, pool_wb, row3(pool_b), row3(pool_scale), w_out_b,
                         **dims)
        xt = _ffn_call(l, xt, mods, g2, w_val, w_gate, 0.5 * ffn_conv_w, row3(0.5 * ffn_conv_b), w_down,
                       last=last, **dims)
    return _final_call(xt, final_norm_g, perm, nb=nb, n_lat=n_lat)
```

```python
import functools
import math

import jax
import jax.numpy as jnp
from jax import lax
from jax.experimental import pallas as pl
from jax.experimental.pallas import tpu as pltpu

F32 = jnp.float32
BF16 = jnp.bfloat16

D_MODEL = 1024
GRID_W = 64
EPS = 1e-6
N_MODS = 6

N_HEADS = 8
Q_RANK = 256
KV_RANK = 128
NOPE_DIM = 64
ROPE_DIM = 32
V_DIM = 64
MLA_W = N_HEADS * V_DIM
SM_SCALE = (NOPE_DIM + ROPE_DIM) ** -0.5
ROPE_THETA = 10000.0
LOG2E = math.log2(math.e)

LRU_W = D_MODEL
LRU_BLOCKS = 8
LRU_BS = LRU_W // LRU_BLOCKS
LRU_CONV = 4
LRU_LEFT = LRU_CONV // 2
LRU_RIGHT = LRU_CONV - 1 - LRU_LEFT
LRU_C = 8.0

POOL_WINDOWS = (2, 4, 8, 16)
POOL_W = D_MODEL // 2
POOL_G = POOL_W // len(POOL_WINDOWS)

D_FF = 2816
FFN_CONV = 3

COL_KV = KV_RANK
COL_KR = COL_KV + ROPE_DIM
COL_UX = COL_KR + LRU_W
COL_Q = COL_UX + Q_RANK
COL_UY = COL_Q + LRU_W
COL_POOL = COL_UY + POOL_W

LANE = 128
HEAD_PAD = LANE
HEADS_W = N_HEADS * HEAD_PAD
PA_CKV, PA_KR, PA_KRP, PA_CQ = 0, 128, 256, 384
PA_ATT = PA_CQ + Q_RANK
PA_SEQ = 2 * LRU_W + POOL_W

ROWS_PERM = 256
ROWS_IN = 512
ROWS_MERGE = 512
ROWS_FFN = 512
ROWS_OUT = 512
Q_TILE = 512
POOL_HALO = max(POOL_WINDOWS) // 2
VMEM_LIMIT = 56 * 1024 * 1024


def _cparams(sem):
    return pltpu.CompilerParams(dimension_semantics=sem, vmem_limit_bytes=VMEM_LIMIT)


def _resident(block_shape, index_map):
    return pl.BlockSpec(block_shape, index_map, pipeline_mode=pl.Buffered(1))


def _rms(x, g):
    return x * lax.rsqrt(jnp.mean(x * x, axis=-1, keepdims=True) + EPS) * g


def _modulate(y, shift, scale, nb):
    r, d = y.shape
    y3 = y.reshape(r // nb, nb, d)
    return (y3 * (1.0 + scale)[None] + shift[None]).reshape(r, d)


def _gate_rows(y, gate, nb):
    r, d = y.shape
    return (y.reshape(r // nb, nb, d) * gate[None]).reshape(r, d)


def _dot(a, b):
    return jnp.dot(a, b, preferred_element_type=F32)


def _row_perm(rows, nb):
    tt = rows // nb
    dst = jnp.arange(rows)
    src = (dst % tt) * nb + dst // tt
    return (src[:, None] == jnp.arange(rows)[None, :]).astype(BF16)


def _tile_of(j, lat_tiles, n_tiles, reverse):
    ctx_tiles = n_tiles - lat_tiles
    if reverse:
        return jnp.where(j < ctx_tiles, n_tiles - 1 - j, lat_tiles - 1 - (j - ctx_tiles))
    return jnp.where(j < ctx_tiles, lat_tiles + j, j - ctx_tiles)


def _segment(t0, tt, n_lat, n_all):
    seg_start = jnp.logical_or(t0 == 0, t0 == n_lat)
    seg_end = jnp.logical_or(t0 + tt == n_lat, t0 + tt == n_all)
    return jnp.where(seg_start, 0.0, 1.0), jnp.where(seg_end, 0.0, 1.0)


def _regroup_rows(perm, y):
    hi = y.astype(BF16)
    r1 = y - hi.astype(F32)
    mid = r1.astype(BF16)
    lo = (r1 - mid.astype(F32)).astype(BF16)
    return (_dot(perm, hi) + _dot(perm, mid)) + _dot(perm, lo)


def _to_time_major_kernel(x_ref, c_ref, perm_t_ref, o_ref, *, lat_tiles):
    nb = x_ref.shape[0]
    ts = ROWS_PERM // nb

    def emit(src_ref):
        for s in range(o_ref.shape[0] // ROWS_PERM):
            rows_bt = jnp.concatenate([src_ref[b, s * ts:(s + 1) * ts, :] for b in range(nb)], axis=0)
            o_ref[s * ROWS_PERM:(s + 1) * ROWS_PERM, :] = _regroup_rows(perm_t_ref[...], rows_bt)

    @pl.when(pl.program_id(0) < lat_tiles)
    def _():
        emit(x_ref)

    @pl.when(pl.program_id(0) >= lat_tiles)
    def _():
        emit(c_ref)


def _to_time_major_call(x, ctx, perm_t):
    nb, n_lat, _ = x.shape
    n_ctx = ctx.shape[1]
    r = ROWS_OUT
    tt = r // nb
    lat_tiles = n_lat // tt
    return pl.pallas_call(
        functools.partial(_to_time_major_kernel, lat_tiles=lat_tiles),
        out_shape=jax.ShapeDtypeStruct(((n_lat + n_ctx) * nb, D_MODEL), F32),
        grid=((n_lat + n_ctx) // tt,),
        in_specs=[
            pl.BlockSpec((nb, tt, D_MODEL), lambda i: (0, jnp.minimum(i, lat_tiles - 1), 0)),
            pl.BlockSpec((nb, tt, D_MODEL), lambda i: (0, jnp.maximum(i - lat_tiles, 0), 0)),
            _resident((ROWS_PERM, ROWS_PERM), lambda i: (0, 0)),
        ],
        out_specs=pl.BlockSpec((r, D_MODEL), lambda i: (i, 0)),
        compiler_params=_cparams(("parallel",)),
        name="to_time_major",
    )(x, ctx, perm_t)


def _ada_kernel(cond_ref, w_ref, b_ref, o_ref):
    c = cond_ref[...]
    s = (c * jax.nn.sigmoid(c)).astype(BF16)
    o_ref[...] = _dot(s, w_ref[...].astype(BF16)) + b_ref[...]


def _ada_call(cond, ada_w, ada_b):
    depth = ada_w.shape[0]
    rows = cond.shape[0]
    n_out = ada_w.shape[2]
    tn = n_out // 4
    return pl.pallas_call(
        _ada_kernel,
        out_shape=jax.ShapeDtypeStruct((depth, rows, n_out), F32),
        grid=(depth, n_out // tn),
        in_specs=[
            pl.BlockSpec((rows, D_MODEL), lambda l, j: (0, 0)),
            pl.BlockSpec((None, D_MODEL, tn), lambda l, j: (l, 0, j)),
            pl.BlockSpec((None, 1, tn), lambda l, j: (l, 0, j)),
        ],
        out_specs=pl.BlockSpec((None, rows, tn), lambda l, j: (l, 0, j)),
        compiler_params=_cparams(("parallel", "parallel")),
        name="ada",
    )(cond, ada_w, ada_b.reshape(depth, 1, n_out))


def _lru_coeffs(u_of, wa_ref, wi_ref, ba_ref, bi_ref, lam_ref, a_s, b_s, fillers=()):
    fillers = list(fillers)
    for blk in range(LRU_BLOCKS):
        if blk < len(fillers):
            fillers[blk]()
        sl = slice(blk * LRU_BS, (blk + 1) * LRU_BS)
        u = u_of(sl)
        ub = u.astype(BF16)
        ta = jnp.tanh(_dot(ub, wa_ref[blk]) + ba_ref[:, sl])
        ti = jnp.tanh(_dot(ub, wi_ref[blk]) + bi_ref[:, sl])
        z = -lam_ref[:, sl]
        softplus = jnp.maximum(z, 0.0) + jnp.log(1.0 + jnp.exp(-jnp.abs(z)))
        ch = (-0.5 * LRU_C * LOG2E) * softplus
        a = jnp.exp2(ta * ch + ch)
        y = 1.0 - a * a
        root = jnp.where(y > 0.0, y * lax.rsqrt(y), 0.0)
        uh = 0.5 * u
        a_s[:, sl] = a
        b_s[:, sl] = (ti * uh + uh) * root


def _lru_gate_specs(l, d):
    vec = _resident((None, None, 1, LRU_W), lambda j: (l, d, 0, 0))
    mat = _resident((None, None, LRU_BLOCKS, LRU_BS, LRU_BS), lambda j: (l, d, 0, 0, 0))
    return [mat, mat, vec, vec, vec]


def _inproj_kernel(x_ref, xr_ref, mod_ref, g1_ref, perm_ref, watt_ref, wseq_ref, kvg_ref, wkv_ref,
                   qg_ref, wq_ref, cos_ref, sin_ref, vone_ref, cw_ref, cb_ref, wa_ref, wi_ref,
                   ba_ref, bi_ref, lam_ref, q_ref, k_ref, v_ref, u_ref, guy_ref, pin_ref, hf_ref,
                   hmod_ref, a_s, b_s, h_s, tail_s, *, nb, n_lat, n_ctx):
    j = pl.program_id(0)
    rows = x_ref.shape[0]
    tt = rows // nb
    n_all = n_lat + n_ctx
    tile = _tile_of(j, n_lat // tt, n_all // tt, False)
    keep_l, keep_r = _segment(tile * tt, tt, n_lat, n_all)
    n_sub = rows // ROWS_PERM
    ts = ROWS_PERM // nb
    shift = mod_ref[:, 0:D_MODEL]
    scale = mod_ref[:, D_MODEL:2 * D_MODEL]
    norm_mod = lambda v: _modulate(_rms(v, g1_ref[...]), shift, scale, nb)

    @pl.when(j == 0)
    def _():
        h_s[...] = jnp.zeros_like(h_s)
        tail_s[...] = jnp.zeros_like(tail_s)

    h = norm_mod(x_ref[...]).astype(BF16)
    hmod_ref[...] = h

    h_next = (norm_mod(xr_ref[...]) * keep_r).astype(BF16)
    ux_ext = _dot(jnp.concatenate([h, h_next], axis=0), wseq_ref[:, 0:LRU_W])

    ext = jnp.concatenate([tail_s[...] * keep_l, ux_ext], axis=0)
    tail_s[...] = ux_ext[rows - LRU_LEFT * nb:rows]

    def conv(sl):
        u = cb_ref[:, sl]
        for tap in range(LRU_CONV):
            u = u + cw_ref[tap:tap + 1, sl] * ext[tap * nb:tap * nb + rows, sl]
        u_ref[:, sl] = u
        return u

    tile_pos = lambda ref: jnp.concatenate(
        [ref[s * ts:(s + 1) * ts] for s in range(n_sub) for _ in range(nb)], axis=0)
    val = {}

    def to_positions(ref, arr):
        for s in range(n_sub):
            for b in range(nb):
                ref[s * ts:(s + 1) * ts, b * HEADS_W:(b + 1) * HEADS_W] = (
                    arr[s * ROWS_PERM + b * ts:s * ROWS_PERM + (b + 1) * ts])

    def f_guy():
        guy_ref[...] = jax.nn.gelu(_dot(h, wseq_ref[:, LRU_W:2 * LRU_W])).astype(guy_ref.dtype)

    def f_pin():
        pin_ref[...] = _dot(h, wseq_ref[:, 2 * LRU_W:])

    def f_att():
        perm = perm_ref[...]
        hp = jnp.concatenate([_dot(perm, h[s * ROWS_PERM:(s + 1) * ROWS_PERM]).astype(BF16)
                              for s in range(n_sub)], axis=0)
        val["p"] = _dot(hp, watt_ref[...])
        val["cos"] = tile_pos(cos_ref)
        val["sin"] = tile_pos(sin_ref)

    def f_kv():
        ckvn = _rms(val["p"][:, PA_CKV:PA_CKV + KV_RANK], kvg_ref[...]).astype(BF16)
        val["kv"] = _dot(ckvn, wkv_ref[...])

    def f_k():
        p = val["p"]
        kr = p[:, PA_KR:PA_KR + LANE] * val["cos"] + p[:, PA_KRP:PA_KRP + LANE] * val["sin"]
        k = val["kv"][:, :HEADS_W] + jnp.concatenate([kr] * N_HEADS, axis=1)
        to_positions(k_ref, k.astype(k_ref.dtype))

    def f_v():
        to_positions(v_ref, (val["kv"][:, HEADS_W:] + vone_ref[...]).astype(v_ref.dtype))

    def f_qq():
        cqn = _rms(val["p"][:, PA_CQ:PA_CQ + Q_RANK], qg_ref[...]).astype(BF16)
        val["qq"] = _dot(cqn, wq_ref[...])

    def f_q():
        qq = val["qq"]
        cos_h = jnp.concatenate([val["cos"]] * N_HEADS, axis=1)
        sin_h = jnp.concatenate([val["sin"]] * N_HEADS, axis=1)
        q = (qq[:, :HEADS_W] + qq[:, HEADS_W:2 * HEADS_W] * cos_h + qq[:, 2 * HEADS_W:] * sin_h
             ) * (SM_SCALE * LOG2E)
        to_positions(q_ref, q.astype(q_ref.dtype))

    _lru_coeffs(conv, wa_ref, wi_ref, ba_ref, bi_ref, lam_ref, a_s, b_s,
                fillers=(f_guy, f_att, f_pin, f_kv, f_k, f_qq, f_v, f_q))

    def step(t, hc):
        rs = pl.ds(pl.multiple_of(t * nb, nb), nb)
        hc = a_s[rs, :] * hc + b_s[rs, :]
        hf_ref[rs, :] = hc.astype(hf_ref.dtype)
        return hc

    h_s[...] = lax.fori_loop(0, tt, step, h_s[...], unroll=8)


def _inproj_call(l, xt, mods, g1, perm, w_att, w_seq, kv_g, w_kv2, q_g, wq2, cos_t, sin_t, v_one,
                 lru_params, *, nb, n_lat, n_ctx):
    rows = xt.shape[0]
    t_all = rows // nb
    r = ROWS_IN
    tt = r // nb
    n_tiles = rows // r
    lat_tiles = (n_lat * nb) // r
    tile = lambda j: _tile_of(j, lat_tiles, n_tiles, False)
    seg = lambda j: jnp.where(tile(j) >= lat_tiles, 1, 0)
    row_spec = lambda w: pl.BlockSpec((r, w), lambda j: (tile(j), 0))
    pos_spec = lambda w: pl.BlockSpec((tt, w), lambda j: (tile(j), 0))
    wspec = lambda a, b: _resident((None, a, b), lambda j: (l, 0, 0))
    return pl.pallas_call(
        functools.partial(_inproj_kernel, nb=nb, n_lat=n_lat, n_ctx=n_ctx),
        out_shape=(
            jax.ShapeDtypeStruct((t_all, nb * HEADS_W), BF16),
            jax.ShapeDtypeStruct((t_all, nb * HEADS_W), BF16),
            jax.ShapeDtypeStruct((t_all, nb * HEADS_W), BF16),
            jax.ShapeDtypeStruct((rows, LRU_W), F32),
            jax.ShapeDtypeStruct((rows, LRU_W), BF16),
            jax.ShapeDtypeStruct((rows, POOL_W), F32),
            jax.ShapeDtypeStruct((rows, LRU_W), BF16),
            jax.ShapeDtypeStruct((rows, D_MODEL), BF16),
        ),
        grid=(n_tiles,),
        in_specs=[
            row_spec(D_MODEL),
            pl.BlockSpec((nb, D_MODEL), lambda j: (jnp.minimum((tile(j) + 1) * tt, t_all - 1), 0)),
            pl.BlockSpec((None, None, nb, N_MODS * D_MODEL), lambda j: (l, seg(j), 0, 0)),
            wspec(1, D_MODEL),
            _resident((ROWS_PERM, ROWS_PERM), lambda j: (0, 0)),
            wspec(D_MODEL, PA_ATT),
            wspec(D_MODEL, PA_SEQ),
            wspec(1, KV_RANK),
            wspec(KV_RANK, 2 * HEADS_W),
            wspec(1, Q_RANK),
            wspec(Q_RANK, 3 * HEADS_W),
            pos_spec(LANE),
            pos_spec(LANE),
            _resident((1, HEADS_W), lambda j: (0, 0)),
            _resident((None, LRU_CONV, LRU_W), lambda j: (l, 0, 0)),
            _resident((None, 1, LRU_W), lambda j: (l, 0, 0)),
        ] + _lru_gate_specs(l, 0),
        out_specs=(pos_spec(nb * HEADS_W), pos_spec(nb * HEADS_W), pos_spec(nb * HEADS_W),
                   row_spec(LRU_W), row_spec(LRU_W), row_spec(POOL_W), row_spec(LRU_W),
                   row_spec(D_MODEL)),
        scratch_shapes=[pltpu.VMEM((r, LRU_W), F32), pltpu.VMEM((r, LRU_W), F32),
                        pltpu.VMEM((nb, LRU_W), F32), pltpu.VMEM((LRU_LEFT * nb, LRU_W), F32)],
        compiler_params=_cparams(("arbitrary",)),
        name="inproj",
    )(xt, xt, mods, g1, perm, w_att, w_seq, kv_g, w_kv2, q_g, wq2, cos_t, sin_t, v_one, *lru_params)


def _attn_kernel(q_ref, k_ref, v_ref, o_ref):
    tq = q_ref.shape[0]
    lane = lax.broadcasted_iota(jnp.int32, (tq, LANE), 1)
    for pair in range(N_HEADS // 2):
        res = []
        for hh in range(2):
            hs = slice((2 * pair + hh) * HEAD_PAD, (2 * pair + hh + 1) * HEAD_PAD)
            s = lax.dot_general(q_ref[:, hs], k_ref[:, hs], (((1,), (1,)), ((), ())),
                                preferred_element_type=F32)
            e = jnp.exp2(s - jnp.max(s, axis=-1, keepdims=True))
            res.append(_dot(e.astype(BF16), v_ref[:, hs]))
        den0 = res[0][:, V_DIM:V_DIM + 1]
        den1 = res[1][:, 0:1]
        o_ref[:, pair * LANE:(pair + 1) * LANE] = jnp.where(
            lane < V_DIM, res[0] * (1.0 / den0), res[1] * (1.0 / den1)).astype(o_ref.dtype)


def _attn_call(q, k, v, *, nb, q_tile, q_off, n_q, k_rows, k_off):
    qo, ko = q_off // q_tile, k_off // k_rows
    return pl.pallas_call(
        _attn_kernel,
        out_shape=jax.ShapeDtypeStruct((n_q, nb * MLA_W), BF16),
        grid=(nb, n_q // q_tile),
        in_specs=[
            pl.BlockSpec((q_tile, HEADS_W), lambda b, i: (i + qo, b)),
            pl.BlockSpec((k_rows, HEADS_W), lambda b, i: (ko, b)),
            pl.BlockSpec((k_rows, HEADS_W), lambda b, i: (ko, b)),
        ],
        out_specs=pl.BlockSpec((q_tile, MLA_W), lambda b, i: (i, b)),
        compiler_params=_cparams(("parallel", "arbitrary")),
        name="attn",
    )(q, k, v)


def _pool_diff(ext, t0, nb, rows, n_lat, n_ctx):
    halo = POOL_HALO * nb

    def shifted_sum(arr, start, k):
        n = arr.shape[0]
        return arr[2 * k:n] + arr[0:n - 2 * k], start + k

    cur, cst = ext[nb:] + ext[:-nb], nb
    sums = {POOL_WINDOWS[0]: (cur, cst)}
    step = nb
    for w in POOL_WINDOWS[1:]:
        cur, cst = shifted_sum(cur, cst, step)
        sums[w] = (cur, cst)
        step *= 2

    t = t0 + lax.broadcasted_iota(jnp.int32, (rows // nb, nb, POOL_G), 0).reshape(rows, POOL_G)
    in_lat = t < n_lat
    t_loc = jnp.where(in_lat, t, t - n_lat)
    n_seg = jnp.where(in_lat, n_lat, n_ctx)
    outs = []
    for g, w in enumerate(POOL_WINDOWS):
        arr, start = sums[w]
        lanes = slice(g * POOL_G, (g + 1) * POOL_G)
        tot = arr[halo - start:halo - start + rows, lanes]
        left = w // 2
        right = w - 1 - left
        cnt = (jnp.minimum(t_loc + right + 1, n_seg) - jnp.maximum(t_loc - left, 0)).astype(F32)
        outs.append(tot / cnt - ext[halo:halo + rows, lanes])
    return outs


def _merge_kernel(x_ref, mod_ref, hmod_ref, perm_ref, attl_ref, attc_ref, hf_ref, guy_ref, u_ref,
                  pin_ref, pl_ref, pr_ref, wa_ref, wi_ref, ba_ref, bi_ref, lam_ref, wg_ref, pm_ref,
                  plru_ref, pp_ref, pw_ref, pb_ref, ps_ref, wo_ref, o_ref, a_s, b_s, h_s, rec_s,
                  *, nb, n_lat, n_ctx):
    j = pl.program_id(0)
    rows = x_ref.shape[0]
    tt = rows // nb
    n_all = n_lat + n_ctx
    lat_tiles = n_lat // tt
    tile = _tile_of(j, lat_tiles, n_all // tt, True)
    t0 = tile * tt
    keep_l, keep_r = _segment(t0, tt, n_lat, n_all)

    @pl.when(j == 0)
    def _():
        h_s[...] = jnp.zeros_like(h_s)

    gate1 = mod_ref[:, 2 * D_MODEL:3 * D_MODEL]
    h = hmod_ref[...]
    val = {}

    def gate(kk):
        return 0.5 * jnp.tanh(_dot(h, wg_ref[:, kk * D_MODEL:(kk + 1) * D_MODEL])) + 0.5

    def f_pool():
        pext = jnp.concatenate([pl_ref[...] * keep_l, pin_ref[...], pr_ref[...] * keep_r], axis=0)
        diffs = _pool_diff(pext, t0, nb, rows, n_lat, n_ctx)
        pooled = [_dot(dg.astype(BF16), pw_ref[g]) for g, dg in enumerate(diffs)]
        val["pool"] = ((jnp.concatenate(pooled, axis=1) + pb_ref[...]) * ps_ref[...]).astype(BF16)

    def f_att():
        is_ctx = jnp.where(tile >= lat_tiles, 1.0, 0.0)
        att_pos = (attl_ref[...].astype(F32) * (1.0 - is_ctx)
                   + attc_ref[...].astype(F32) * is_ctx).astype(BF16)
        ts = ROWS_PERM // nb
        perm_t = perm_ref[...]
        val["att"] = jnp.concatenate([
            _dot(perm_t, jnp.concatenate([att_pos[s * ts:(s + 1) * ts, b * MLA_W:(b + 1) * MLA_W]
                                          for b in range(nb)], axis=0)).astype(BF16)
            for s in range(rows // ROWS_PERM)], axis=0)

    def f_m_att():
        val["m"] = gate(0) * _dot(val["att"], pm_ref[...])

    def f_m_pool():
        val["m"] = val["m"] + gate(2) * _dot(val["pool"], pp_ref[...])

    def f_g_rec():
        val["g_rec"] = gate(1)

    _lru_coeffs(lambda sl: u_ref[:, sl], wa_ref, wi_ref, ba_ref, bi_ref, lam_ref, a_s, b_s,
                fillers=(f_att, f_pool, f_m_att, f_m_pool, f_g_rec))
    m, g_rec = val["m"], val["g_rec"]

    def step(s, hc):
        rs = pl.ds(pl.multiple_of((tt - 1 - s) * nb, nb), nb)
        hc = a_s[rs, :] * hc + b_s[rs, :]
        rec_s[rs, :] = ((hf_ref[rs, :].astype(F32) + hc)
                        * guy_ref[rs, :].astype(F32)).astype(rec_s.dtype)
        return hc

    h_s[...] = lax.fori_loop(0, tt, step, h_s[...], unroll=8)

    m = m + g_rec * _dot(rec_s[...], plru_ref[...])
    y = _dot(m.astype(BF16), wo_ref[...])
    o_ref[...] = x_ref[...] + _gate_rows(y, gate1, nb)


def _merge_call(l, xt, mods, hmod, perm_t, att_l, att_c, hf, guy, u, pin, gate_params, w_g, p_mla,
                p_lru, p_pool, pool_w, pool_b, pool_scale, w_out, *, nb, n_lat, n_ctx):
    rows = xt.shape[0]
    r = ROWS_MERGE
    tt = r // nb
    hb = POOL_HALO * nb
    n_tiles = rows // r
    lat_tiles = (n_lat * nb) // r
    tile = lambda j: _tile_of(j, lat_tiles, n_tiles, True)
    seg = lambda j: jnp.where(tile(j) >= lat_tiles, 1, 0)
    row_spec = lambda w: pl.BlockSpec((r, w), lambda j: (tile(j), 0))
    wspec = lambda a, b: _resident((None, a, b), lambda j: (l, 0, 0))
    return pl.pallas_call(
        functools.partial(_merge_kernel, nb=nb, n_lat=n_lat, n_ctx=n_ctx),
        out_shape=jax.ShapeDtypeStruct((rows, D_MODEL), F32),
        grid=(n_tiles,),
        in_specs=[
            row_spec(D_MODEL),
            pl.BlockSpec((None, None, nb, N_MODS * D_MODEL), lambda j: (l, seg(j), 0, 0)),
            row_spec(D_MODEL),
            _resident((ROWS_PERM, ROWS_PERM), lambda j: (0, 0)),
            pl.BlockSpec((tt, nb * MLA_W), lambda j: (jnp.minimum(tile(j), lat_tiles - 1), 0)),
            pl.BlockSpec((tt, nb * MLA_W), lambda j: (jnp.maximum(tile(j) - lat_tiles, 0), 0)),
            row_spec(LRU_W),
            row_spec(LRU_W),
            row_spec(LRU_W),
            row_spec(POOL_W),
            pl.BlockSpec((hb, POOL_W), lambda j: (jnp.maximum(tile(j) * (r // hb) - 1, 0), 0)),
            pl.BlockSpec((hb, POOL_W),
                         lambda j: (jnp.minimum((tile(j) + 1) * (r // hb), rows // hb - 1), 0)),
        ] + _lru_gate_specs(l, 1) + [
            wspec(D_MODEL, 3 * D_MODEL),
            wspec(MLA_W, D_MODEL),
            wspec(LRU_W, D_MODEL),
            wspec(POOL_W, D_MODEL),
            _resident((None, len(POOL_WINDOWS), POOL_G, POOL_G), lambda j: (l, 0, 0, 0)),
            wspec(1, POOL_W),
            wspec(1, POOL_W),
            wspec(D_MODEL, D_MODEL),
        ],
        out_specs=row_spec(D_MODEL),
        scratch_shapes=[pltpu.VMEM((r, LRU_W), F32), pltpu.VMEM((r, LRU_W), F32),
                        pltpu.VMEM((nb, LRU_W), F32), pltpu.VMEM((r, LRU_W), BF16)],
        compiler_params=_cparams(("arbitrary",)),
        name="merge",
    )(xt, mods, hmod, perm_t, att_l, att_c, hf, guy, u, pin, pin, pin, *gate_params, w_g,
      p_mla, p_lru, p_pool, pool_w, pool_b, pool_scale, w_out)


def _ffn_kernel(x_ref, xl_ref, xr_ref, mod_ref, g2_ref, wv_ref, wgt_ref, cw_ref, cb_ref, wd_ref,
                o_ref, *, nb, n_lat, n_ctx):
    rows = x_ref.shape[0]
    tt = rows // nb
    t0 = pl.program_id(0) * tt

    x = x_ref[...]
    shift = mod_ref[:, 3 * D_MODEL:4 * D_MODEL]
    scale = mod_ref[:, 4 * D_MODEL:5 * D_MODEL]
    gate2 = mod_ref[:, 5 * D_MODEL:6 * D_MODEL]

    keep_l, keep_r = _segment(t0, tt, n_lat, n_lat + n_ctx)
    norm_mod = lambda v: _modulate(_rms(v, g2_ref[...]), shift, scale, nb)
    he = jnp.concatenate([norm_mod(xl_ref[...]) * keep_l, norm_mod(x), norm_mod(xr_ref[...]) * keep_r],
                         axis=0).astype(BF16)
    gate = _dot(he, wgt_ref[...])
    ch = (cw_ref[0:1, :] * gate[0:rows] + cw_ref[1:2, :] * gate[nb:nb + rows]
          + cw_ref[2:3, :] * gate[2 * nb:2 * nb + rows] + cb_ref[...])
    val = _dot(he[nb:nb + rows], wv_ref[...])
    act = ((ch * jnp.tanh(ch) + ch) * val).astype(BF16)
    o_ref[...] = x + _gate_rows(_dot(act, wd_ref[...]), gate2, nb)


def _ffn_call(l, xt, mods, g2, w_val, w_gate, conv_w, conv_b, w_down, *, nb, n_lat, n_ctx, last):
    rows = xt.shape[0]
    r = ROWS_FFN
    tt = r // nb
    lat_tiles = (n_lat * nb) // r
    live_pos = n_lat if last else n_lat + n_ctx
    seg = lambda i: jnp.where(i >= lat_tiles, 1, 0)
    wspec = lambda a, b: _resident((None, a, b), lambda i: (l, 0, 0))
    return pl.pallas_call(
        functools.partial(_ffn_kernel, nb=nb, n_lat=n_lat, n_ctx=n_ctx),
        out_shape=jax.ShapeDtypeStruct((rows, D_MODEL), F32),
        grid=(live_pos * nb // r,),
        in_specs=[
            pl.BlockSpec((r, D_MODEL), lambda i: (i, 0)),
            pl.BlockSpec((nb, D_MODEL), lambda i: (jnp.maximum(i * tt - 1, 0), 0)),
            pl.BlockSpec((nb, D_MODEL), lambda i: (jnp.minimum((i + 1) * tt, live_pos - 1), 0)),
            pl.BlockSpec((None, None, nb, N_MODS * D_MODEL), lambda i: (l, seg(i), 0, 0)),
            wspec(1, D_MODEL),
            wspec(D_MODEL, D_FF),
            wspec(D_MODEL, D_FF),
            wspec(FFN_CONV, D_FF),
            wspec(1, D_FF),
            wspec(D_FF, D_MODEL),
        ],
        out_specs=pl.BlockSpec((r, D_MODEL), lambda i: (i, 0)),
        compiler_params=_cparams(("parallel",)),
        name="ffn",
    )(xt, xt, xt, mods, g2, w_val, w_gate, conv_w, conv_b, w_down)


def _final_kernel(x_ref, g_ref, perm_ref, o_ref):
    nb = o_ref.shape[0]
    ts = ROWS_PERM // nb
    y = _rms(x_ref[...], g_ref[...])
    for s in range(x_ref.shape[0] // ROWS_PERM):
        yp = _regroup_rows(perm_ref[...], y[s * ROWS_PERM:(s + 1) * ROWS_PERM])
        for b in range(nb):
            o_ref[b, s * ts:(s + 1) * ts, :] = yp[b * ts:(b + 1) * ts]


def _final_call(xt, g, perm, *, nb, n_lat):
    r = ROWS_OUT
    tt = r // nb
    return pl.pallas_call(
        _final_kernel,
        out_shape=jax.ShapeDtypeStruct((nb, n_lat, D_MODEL), F32),
        grid=(n_lat * nb // r,),
        in_specs=[
            pl.BlockSpec((r, D_MODEL), lambda i: (i, 0)),
            _resident((1, D_MODEL), lambda i: (0, 0)),
            _resident((ROWS_PERM, ROWS_PERM), lambda i: (0, 0)),
        ],
        out_specs=pl.BlockSpec((nb, tt, D_MODEL), lambda i: (0, i, 0)),
        compiler_params=_cparams(("parallel",)),
        name="final_norm",
    )(xt, g.reshape(1, D_MODEL), perm)


def _rope_tables(n_lat, n_ctx):
    half = ROPE_DIM // 2
    inv = ROPE_THETA ** (-jnp.arange(0, half, 2, dtype=F32) / half)
    t = jnp.arange(n_lat)
    ang_r = (t // GRID_W).astype(F32)[:, None] * inv
    ang_c = (t % GRID_W).astype(F32)[:, None] * inv
    cr, sr, cc, sc = jnp.cos(ang_r), jnp.sin(ang_r), jnp.cos(ang_c), jnp.sin(ang_c)
    cos32 = jnp.concatenate([cr, cr, cc, cc], axis=1)
    sin32 = jnp.concatenate([-sr, sr, -sc, sc], axis=1)
    cos32 = jnp.concatenate([cos32, jnp.ones((n_ctx, ROPE_DIM), F32)], axis=0)
    sin32 = jnp.concatenate([sin32, jnp.zeros((n_ctx, ROPE_DIM), F32)], axis=0)
    pad = lambda a: jnp.pad(a, ((0, 0), (NOPE_DIM, HEAD_PAD - NOPE_DIM - ROPE_DIM)))
    return pad(cos32), pad(sin32)


def _rope_partner(w):
    q = ROPE_DIM // 4
    return jnp.concatenate([w[..., q:2 * q], w[..., 0:q], w[..., 3 * q:4 * q], w[..., 2 * q:3 * q]],
                           axis=-1)


def _pack_weights(w_in, w_uq, w_ukv):
    depth = w_in.shape[0]
    zeros = lambda n: jnp.zeros((depth, D_MODEL, n), w_in.dtype)
    w_kr = w_in[:, :, COL_KV:COL_KR]
    tail = HEAD_PAD - NOPE_DIM - ROPE_DIM
    kr_grp = lambda w: jnp.concatenate([zeros(NOPE_DIM), w, zeros(tail)], axis=-1)
    w_att = jnp.concatenate([
        w_in[:, :, :COL_KV], kr_grp(w_kr), kr_grp(_rope_partner(w_kr)),
        w_in[:, :, COL_UX:COL_Q]], axis=-1).astype(BF16)
    w_seq = jnp.concatenate([w_in[:, :, COL_KR:COL_UX], w_in[:, :, COL_Q:COL_UY],
                             w_in[:, :, COL_UY:COL_POOL]], axis=-1).astype(BF16)
    w_g = (0.5 * w_in[:, :, COL_POOL:]).astype(BF16)

    uq = w_uq.reshape(depth, Q_RANK, N_HEADS, NOPE_DIM + ROPE_DIM)
    qn, qr = uq[..., :NOPE_DIM], uq[..., NOPE_DIM:]
    padh = lambda a, lo: jnp.pad(a, ((0, 0), (0, 0), (0, 0), (lo, HEAD_PAD - lo - a.shape[-1])))
    flat = lambda a: a.reshape(depth, a.shape[1], HEADS_W)
    wq2 = jnp.concatenate([flat(padh(qn, 0)), flat(padh(qr, NOPE_DIM)),
                           flat(padh(_rope_partner(qr), NOPE_DIM))], axis=-1).astype(BF16)

    ukv = w_ukv.reshape(depth, KV_RANK, N_HEADS, NOPE_DIM + V_DIM)
    kn, vv = ukv[..., :NOPE_DIM], ukv[..., NOPE_DIM:]
    v_even = padh(vv, 0)
    v_odd = padh(vv, V_DIM)
    odd = (jnp.arange(N_HEADS) % 2 == 1)[None, None, :, None]
    w_kv2 = jnp.concatenate([flat(padh(kn, 0)), flat(jnp.where(odd, v_odd, v_even))],
                            axis=-1).astype(BF16)
    lane = jnp.arange(HEADS_W) % HEAD_PAD
    head_odd = (jnp.arange(HEADS_W) // HEAD_PAD) % 2 == 1
    v_one = jnp.where(head_odd, lane == 0, lane == V_DIM).astype(F32)[None, :]
    return w_att, w_seq, w_g, wq2, w_kv2, v_one


def kernel(x, c, ctx, c_ctx, ada_w, ada_b, norm1_g, norm2_g, w_in, q_norm_g, w_uq, kv_norm_g, w_ukv,
           lru_conv_w, lru_conv_b, lru_wa, lru_ba, lru_wi, lru_bi, lru_lambda, pool_w, pool_b,
           pool_scale, proj_mla, proj_lru, proj_pool, w_out, ffn_up, ffn_conv_w, ffn_conv_b,
           ffn_down, final_norm_g):
    nb, n_lat, _ = x.shape
    n_ctx = ctx.shape[1]
    depth = ada_w.shape[0]
    t_all = n_lat + n_ctx
    assert nb % 8 == 0 and ROWS_PERM % nb == 0
    assert n_lat % Q_TILE == 0 and n_lat % n_ctx == 0
    for r in (ROWS_IN, ROWS_MERGE, ROWS_FFN, ROWS_OUT):
        assert (n_ctx * nb) % r == 0 and (n_lat * nb) % r == 0
        assert r % (POOL_HALO * nb) == 0 and r % ROWS_PERM == 0

    perm = _row_perm(ROWS_PERM, nb)
    perm_t = perm.T
    xt = _to_time_major_call(x, ctx, perm_t)

    cond_rows = -(-(nb + 1) // 8) * 8
    cond = jnp.zeros((cond_rows, D_MODEL), F32).at[:nb].set(c).at[nb].set(c_ctx)
    mods = _ada_call(cond, ada_w, ada_b)
    mods = jnp.stack([mods[:, :nb],
                      jnp.broadcast_to(mods[:, nb:nb + 1], (depth, nb, N_MODS * D_MODEL))], axis=1)

    w_att, w_seq, w_g, wq2, w_kv2, v_one = _pack_weights(w_in, w_uq, w_ukv)
    cos_t, sin_t = _rope_tables(n_lat, n_ctx)
    row3 = lambda a: a.reshape(depth, 1, a.shape[-1])
    vec4 = lambda a: a.reshape(depth, 2, 1, a.shape[-1])
    g1, g2 = row3(norm1_g), row3(norm2_g)
    kv_g, q_g = row3(kv_norm_g), row3(q_norm_g)
    conv_params = (lru_conv_w, row3(lru_conv_b))
    gate_params = ((0.5 * lru_wa).astype(BF16), (0.5 * lru_wi).astype(BF16), vec4(0.5 * lru_ba),
                   vec4(0.5 * lru_bi), vec4(lru_lambda))
    p_mla, p_lru, p_pool = proj_mla.astype(BF16), proj_lru.astype(BF16), proj_pool.astype(BF16)
    pool_wb, w_out_b = pool_w.astype(BF16), w_out.astype(BF16)
    w_val, w_gate = ffn_up[:, :, :D_FF].astype(BF16), ffn_up[:, :, D_FF:].astype(BF16)
    w_down = ffn_down.astype(BF16)

    dims = dict(nb=nb, n_lat=n_lat, n_ctx=n_ctx)
    for l in range(depth):
        last = l == depth - 1
        q, k, v, u, guy, pin, hf, hmod = _inproj_call(l, xt, mods, g1, perm, w_att, w_seq, kv_g,
                                                      w_kv2, q_g, wq2, cos_t, sin_t, v_one,
                                                      conv_params + gate_params, **dims)
        att_l = _attn_call(q, k, v, nb=nb, q_tile=Q_TILE, q_off=0, n_q=n_lat, k_rows=t_all, k_off=0)
        att_c = att_l if last else _attn_call(q, k, v, nb=nb, q_tile=n_ctx, q_off=n_lat, n_q=n_ctx,
                                              k_rows=n_ctx, k_off=n_lat)
        xt = _merge_call(l, xt, mods, hmod, perm_t, att_l, att_c, hf, guy, u, pin, gate_params, w_g,
                         p_mla, p_lru, p_pool, pool_wb, row3(pool_b), row3(pool_scale), w_out_b,
                         **dims)
        xt = _ffn_call(l, xt, mods, g2, w_val, w_gate, 0.5 * ffn_conv_w, row3(0.5 * ffn_conv_b), w_down,
                       last=last, **dims)
    return _final_call(xt, final_norm_g, perm, nb=nb, n_lat=n_lat)
```

```python
import functools
import math

import jax
import jax.numpy as jnp
from jax import lax
from jax.experimental import pallas as pl
from jax.experimental.pallas import tpu as pltpu

F32 = jnp.float32
BF16 = jnp.bfloat16

D_MODEL = 1024
GRID_W = 64
EPS = 1e-6
N_MODS = 6

N_HEADS = 8
Q_RANK = 256
KV_RANK = 128
NOPE_DIM = 64
ROPE_DIM = 32
V_DIM = 64
MLA_W = N_HEADS * V_DIM
SM_SCALE = (NOPE_DIM + ROPE_DIM) ** -0.5
ROPE_THETA = 10000.0
LOG2E = math.log2(math.e)

LRU_W = D_MODEL
LRU_BLOCKS = 8
LRU_BS = LRU_W // LRU_BLOCKS
LRU_CONV = 4
LRU_LEFT = LRU_CONV // 2
LRU_RIGHT = LRU_CONV - 1 - LRU_LEFT
LRU_C = 8.0

POOL_WINDOWS = (2, 4, 8, 16)
POOL_W = D_MODEL // 2
POOL_G = POOL_W // len(POOL_WINDOWS)

D_FF = 2816
FFN_CONV = 3

COL_KV = KV_RANK
COL_KR = COL_KV + ROPE_DIM
COL_UX = COL_KR + LRU_W
COL_Q = COL_UX + Q_RANK
COL_UY = COL_Q + LRU_W
COL_POOL = COL_UY + POOL_W

LANE = 128
HEAD_PAD = LANE
HEADS_W = N_HEADS * HEAD_PAD
PA_CKV, PA_KR, PA_KRP, PA_CQ = 0, 128, 256, 384
PA_ATT = PA_CQ + Q_RANK
PA_SEQ = 2 * LRU_W + POOL_W

ROWS_PERM = 256
ROWS_IN = 512
ROWS_MERGE = 512
ROWS_FFN = 512
ROWS_OUT = 512
Q_TILE = 512
POOL_HALO = max(POOL_WINDOWS) // 2
VMEM_LIMIT = 56 * 1024 * 1024


def _cparams(sem):
    return pltpu.CompilerParams(dimension_semantics=sem, vmem_limit_bytes=VMEM_LIMIT)


def _resident(block_shape, index_map):
    return pl.BlockSpec(block_shape, index_map, pipeline_mode=pl.Buffered(1))


def _rms(x, g):
    return x * lax.rsqrt(jnp.mean(x * x, axis=-1, keepdims=True) + EPS) * g


def _modulate(y, shift, scale, nb):
    r, d = y.shape
    y3 = y.reshape(r // nb, nb, d)
    return (y3 * (1.0 + scale)[None] + shift[None]).reshape(r, d)


def _gate_rows(y, gate, nb):
    r, d = y.shape
    return (y.reshape(r // nb, nb, d) * gate[None]).reshape(r, d)


def _dot(a, b):
    return jnp.dot(a, b, preferred_element_type=F32)


def _row_perm(rows, nb):
    tt = rows // nb
    dst = jnp.arange(rows)
    src = (dst % tt) * nb + dst // tt
    return (src[:, None] == jnp.arange(rows)[None, :]).astype(BF16)


def _tile_of(j, lat_tiles, n_tiles, reverse):
    ctx_tiles = n_tiles - lat_tiles
    if reverse:
        return jnp.where(j < ctx_tiles, n_tiles - 1 - j, lat_tiles - 1 - (j - ctx_tiles))
    return jnp.where(j < ctx_tiles, lat_tiles + j, j - ctx_tiles)


def _segment(t0, tt, n_lat, n_all):
    seg_start = jnp.logical_or(t0 == 0, t0 == n_lat)
    seg_end = jnp.logical_or(t0 + tt == n_lat, t0 + tt == n_all)
    return jnp.where(seg_start, 0.0, 1.0), jnp.where(seg_end, 0.0, 1.0)


def _regroup_rows(perm, y):
    hi = y.astype(BF16)
    r1 = y - hi.astype(F32)
    mid = r1.astype(BF16)
    lo = (r1 - mid.astype(F32)).astype(BF16)
    return (_dot(perm, hi) + _dot(perm, mid)) + _dot(perm, lo)


def _to_time_major_kernel(x_ref, c_ref, perm_t_ref, o_ref, *, lat_tiles):
    nb = x_ref.shape[0]
    ts = ROWS_PERM // nb

    def emit(src_ref):
        for s in range(o_ref.shape[0] // ROWS_PERM):
            rows_bt = jnp.concatenate([src_ref[b, s * ts:(s + 1) * ts, :] for b in range(nb)], axis=0)
            o_ref[s * ROWS_PERM:(s + 1) * ROWS_PERM, :] = _regroup_rows(perm_t_ref[...], rows_bt)

    @pl.when(pl.program_id(0) < lat_tiles)
    def _():
        emit(x_ref)

    @pl.when(pl.program_id(0) >= lat_tiles)
    def _():
        emit(c_ref)


def _to_time_major_call(x, ctx, perm_t):
    nb, n_lat, _ = x.shape
    n_ctx = ctx.shape[1]
    r = ROWS_OUT
    tt = r // nb
    lat_tiles = n_lat // tt
    return pl.pallas_call(
        functools.partial(_to_time_major_kernel, lat_tiles=lat_tiles),
        out_shape=jax.ShapeDtypeStruct(((n_lat + n_ctx) * nb, D_MODEL), F32),
        grid=((n_lat + n_ctx) // tt,),
        in_specs=[
            pl.BlockSpec((nb, tt, D_MODEL), lambda i: (0, jnp.minimum(i, lat_tiles - 1), 0)),
            pl.BlockSpec((nb, tt, D_MODEL), lambda i: (0, jnp.maximum(i - lat_tiles, 0), 0)),
            _resident((ROWS_PERM, ROWS_PERM), lambda i: (0, 0)),
        ],
        out_specs=pl.BlockSpec((r, D_MODEL), lambda i: (i, 0)),
        compiler_params=_cparams(("parallel",)),
        name="to_time_major",
    )(x, ctx, perm_t)


def _ada_kernel(cond_ref, w_ref, b_ref, o_ref):
    c = cond_ref[...]
    s = (c * jax.nn.sigmoid(c)).astype(BF16)
    o_ref[...] = _dot(s, w_ref[...].astype(BF16)) + b_ref[...]


def _ada_call(cond, ada_w, ada_b):
    depth = ada_w.shape[0]
    rows = cond.shape[0]
    n_out = ada_w.shape[2]
    tn = n_out // 4
    return pl.pallas_call(
        _ada_kernel,
        out_shape=jax.ShapeDtypeStruct((depth, rows, n_out), F32),
        grid=(depth, n_out // tn),
        in_specs=[
            pl.BlockSpec((rows, D_MODEL), lambda l, j: (0, 0)),
            pl.BlockSpec((None, D_MODEL, tn), lambda l, j: (l, 0, j)),
            pl.BlockSpec((None, 1, tn), lambda l, j: (l, 0, j)),
        ],
        out_specs=pl.BlockSpec((None, rows, tn), lambda l, j: (l, 0, j)),
        compiler_params=_cparams(("parallel", "parallel")),
        name="ada",
    )(cond, ada_w, ada_b.reshape(depth, 1, n_out))


def _lru_coeffs(u_of, wa_ref, wi_ref, ba_ref, bi_ref, lam_ref, a_s, b_s, fillers=()):
    fillers = list(fillers)
    for blk in range(LRU_BLOCKS):
        if blk < len(fillers):
            fillers[blk]()
        sl = slice(blk * LRU_BS, (blk + 1) * LRU_BS)
        u = u_of(sl)
        ub = u.astype(BF16)
        ta = jnp.tanh(_dot(ub, wa_ref[blk]) + ba_ref[:, sl])
        ti = jnp.tanh(_dot(ub, wi_ref[blk]) + bi_ref[:, sl])
        z = -lam_ref[:, sl]
        softplus = jnp.maximum(z, 0.0) + jnp.log(1.0 + jnp.exp(-jnp.abs(z)))
        ch = (-0.5 * LRU_C * LOG2E) * softplus
        a = jnp.exp2(ta * ch + ch)
        y = 1.0 - a * a
        root = jnp.where(y > 0.0, y * lax.rsqrt(y), 0.0)
        uh = 0.5 * u
        a_s[:, sl] = a
        b_s[:, sl] = (ti * uh + uh) * root


def _lru_gate_specs(l, d):
    vec = _resident((None, None, 1, LRU_W), lambda j: (l, d, 0, 0))
    mat = _resident((None, None, LRU_BLOCKS, LRU_BS, LRU_BS), lambda j: (l, d, 0, 0, 0))
    return [mat, mat, vec, vec, vec]


def _inproj_kernel(x_ref, xr_ref, mod_ref, g1_ref, perm_ref, watt_ref, wseq_ref, kvg_ref, wkv_ref,
                   qg_ref, wq_ref, cos_ref, sin_ref, vone_ref, cw_ref, cb_ref, wa_ref, wi_ref,
                   ba_ref, bi_ref, lam_ref, q_ref, k_ref, v_ref, u_ref, guy_ref, pin_ref, hf_ref,
                   hmod_ref, a_s, b_s, h_s, tail_s, *, nb, n_lat, n_ctx):
    j = pl.program_id(0)
    rows = x_ref.shape[0]
    tt = rows // nb
    n_all = n_lat + n_ctx
    tile = _tile_of(j, n_lat // tt, n_all // tt, False)
    keep_l, keep_r = _segment(tile * tt, tt, n_lat, n_all)
    n_sub = rows // ROWS_PERM
    ts = ROWS_PERM // nb
    shift = mod_ref[:, 0:D_MODEL]
    scale = mod_ref[:, D_MODEL:2 * D_MODEL]
    norm_mod = lambda v: _modulate(_rms(v, g1_ref[...]), shift, scale, nb)

    @pl.when(j == 0)
    def _():
        h_s[...] = jnp.zeros_like(h_s)
        tail_s[...] = jnp.zeros_like(tail_s)

    h = norm_mod(x_ref[...]).astype(BF16)
    hmod_ref[...] = h

    h_next = (norm_mod(xr_ref[...]) * keep_r).astype(BF16)
    ux_ext = _dot(jnp.concatenate([h, h_next], axis=0), wseq_ref[:, 0:LRU_W])

    ext = jnp.concatenate([tail_s[...] * keep_l, ux_ext], axis=0)
    tail_s[...] = ux_ext[rows - LRU_LEFT * nb:rows]

    def conv(sl):
        u = cb_ref[:, sl]
        for tap in range(LRU_CONV):
            u = u + cw_ref[tap:tap + 1, sl] * ext[tap * nb:tap * nb + rows, sl]
        u_ref[:, sl] = u
        return u

    tile_pos = lambda ref: jnp.concatenate(
        [ref[s * ts:(s + 1) * ts] for s in range(n_sub) for _ in range(nb)], axis=0)
    val = {}

    def to_positions(ref, arr):
        for s in range(n_sub):
            for b in range(nb):
                ref[s * ts:(s + 1) * ts, b * HEADS_W:(b + 1) * HEADS_W] = (
                    arr[s * ROWS_PERM + b * ts:s * ROWS_PERM + (b + 1) * ts])

    def f_guy():
        guy_ref[...] = jax.nn.gelu(_dot(h, wseq_ref[:, LRU_W:2 * LRU_W])).astype(guy_ref.dtype)

    def f_pin():
        pin_ref[...] = _dot(h, wseq_ref[:, 2 * LRU_W:])

    def f_att():
        perm = perm_ref[...]
        hp = jnp.concatenate([_dot(perm, h[s * ROWS_PERM:(s + 1) * ROWS_PERM]).astype(BF16)
                              for s in range(n_sub)], axis=0)
        val["p"] = _dot(hp, watt_ref[...])
        val["cos"] = tile_pos(cos_ref)
        val["sin"] = tile_pos(sin_ref)

    def f_kv():
        ckvn = _rms(val["p"][:, PA_CKV:PA_CKV + KV_RANK], kvg_ref[...]).astype(BF16)
        val["kv"] = _dot(ckvn, wkv_ref[...])

    def f_k():
        p = val["p"]
        kr = p[:, PA_KR:PA_KR + LANE] * val["cos"] + p[:, PA_KRP:PA_KRP + LANE] * val["sin"]
        k = val["kv"][:, :HEADS_W] + jnp.concatenate([kr] * N_HEADS, axis=1)
        to_positions(k_ref, k.astype(k_ref.dtype))

    def f_v():
        to_positions(v_ref, (val["kv"][:, HEADS_W:] + vone_ref[...]).astype(v_ref.dtype))

    def f_qq():
        cqn = _rms(val["p"][:, PA_CQ:PA_CQ + Q_RANK], qg_ref[...]).astype(BF16)
        val["qq"] = _dot(cqn, wq_ref[...])

    def f_q():
        qq = val["qq"]
        cos_h = jnp.concatenate([val["cos"]] * N_HEADS, axis=1)
        sin_h = jnp.concatenate([val["sin"]] * N_HEADS, axis=1)
        q = (qq[:, :HEADS_W] + qq[:, HEADS_W:2 * HEADS_W] * cos_h + qq[:, 2 * HEADS_W:] * sin_h
             ) * (SM_SCALE * LOG2E)
        to_positions(q_ref, q.astype(q_ref.dtype))

    _lru_coeffs(conv, wa_ref, wi_ref, ba_ref, bi_ref, lam_ref, a_s, b_s,
                fillers=(f_guy, f_att, f_pin, f_kv, f_k, f_qq, f_v, f_q))

    def step(t, hc):
        rs = pl.ds(pl.multiple_of(t * nb, nb), nb)
        hc = a_s[rs, :] * hc + b_s[rs, :]
        hf_ref[rs, :] = hc.astype(hf_ref.dtype)
        return hc

    h_s[...] = lax.fori_loop(0, tt, step, h_s[...], unroll=8)


def _inproj_call(l, xt, mods, g1, perm, w_att, w_seq, kv_g, w_kv2, q_g, wq2, cos_t, sin_t, v_one,
                 lru_params, *, nb, n_lat, n_ctx):
    rows = xt.shape[0]
    t_all = rows // nb
    r = ROWS_IN
    tt = r // nb
    n_tiles = rows // r
    lat_tiles = (n_lat * nb) // r
    tile = lambda j: _tile_of(j, lat_tiles, n_tiles, False)
    seg = lambda j: jnp.where(tile(j) >= lat_tiles, 1, 0)
    row_spec = lambda w: pl.BlockSpec((r, w), lambda j: (tile(j), 0))
    pos_spec = lambda w: pl.BlockSpec((tt, w), lambda j: (tile(j), 0))
    wspec = lambda a, b: _resident((None, a, b), lambda j: (l, 0, 0))
    return pl.pallas_call(
        functools.partial(_inproj_kernel, nb=nb, n_lat=n_lat, n_ctx=n_ctx),
        out_shape=(
            jax.ShapeDtypeStruct((t_all, nb * HEADS_W), BF16),
            jax.ShapeDtypeStruct((t_all, nb * HEADS_W), BF16),
            jax.ShapeDtypeStruct((t_all, nb * HEADS_W), BF16),
            jax.ShapeDtypeStruct((rows, LRU_W), F32),
            jax.ShapeDtypeStruct((rows, LRU_W), BF16),
            jax.ShapeDtypeStruct((rows, POOL_W), F32),
            jax.ShapeDtypeStruct((rows, LRU_W), BF16),
            jax.ShapeDtypeStruct((rows, D_MODEL), BF16),
        ),
        grid=(n_tiles,),
        in_specs=[
            row_spec(D_MODEL),
            pl.BlockSpec((nb, D_MODEL), lambda j: (jnp.minimum((tile(j) + 1) * tt, t_all - 1), 0)),
            pl.BlockSpec((None, None, nb, N_MODS * D_MODEL), lambda j: (l, seg(j), 0, 0)),
            wspec(1, D_MODEL),
            _resident((ROWS_PERM, ROWS_PERM), lambda j: (0, 0)),
            wspec(D_MODEL, PA_ATT),
            wspec(D_MODEL, PA_SEQ),
            wspec(1, KV_RANK),
            wspec(KV_RANK, 2 * HEADS_W),
            wspec(1, Q_RANK),
            wspec(Q_RANK, 3 * HEADS_W),
            pos_spec(LANE),
            pos_spec(LANE),
            _resident((1, HEADS_W), lambda j: (0, 0)),
            _resident((None, LRU_CONV, LRU_W), lambda j: (l, 0, 0)),
            _resident((None, 1, LRU_W), lambda j: (l, 0, 0)),
        ] + _lru_gate_specs(l, 0),
        out_specs=(pos_spec(nb * HEADS_W), pos_spec(nb * HEADS_W), pos_spec(nb * HEADS_W),
                   row_spec(LRU_W), row_spec(LRU_W), row_spec(POOL_W), row_spec(LRU_W),
                   row_spec(D_MODEL)),
        scratch_shapes=[pltpu.VMEM((r, LRU_W), F32), pltpu.VMEM((r, LRU_W), F32),
                        pltpu.VMEM((nb, LRU_W), F32), pltpu.VMEM((LRU_LEFT * nb, LRU_W), F32)],
        compiler_params=_cparams(("arbitrary",)),
        name="inproj",
    )(xt, xt, mods, g1, perm, w_att, w_seq, kv_g, w_kv2, q_g, wq2, cos_t, sin_t, v_one, *lru_params)


def _attn_kernel(q_ref, k_ref, v_ref, o_ref):
    tq = q_ref.shape[0]
    lane = lax.broadcasted_iota(jnp.int32, (tq, LANE), 1)
    for pair in range(N_HEADS // 2):
        res = []
        for hh in range(2):
            hs = slice((2 * pair + hh) * HEAD_PAD, (2 * pair + hh + 1) * HEAD_PAD)
            s = lax.dot_general(q_ref[:, hs], k_ref[:, hs], (((1,), (1,)), ((), ())),
                                preferred_element_type=F32)
            e = jnp.exp2(s - jnp.max(s, axis=-1, keepdims=True))
            res.append(_dot(e.astype(BF16), v_ref[:, hs]))
        den0 = res[0][:, V_DIM:V_DIM + 1]
        den1 = res[1][:, 0:1]
        o_ref[:, pair * LANE:(pair + 1) * LANE] = jnp.where(
            lane < V_DIM, res[0] * (1.0 / den0), res[1] * (1.0 / den1)).astype(o_ref.dtype)


def _attn_call(q, k, v, *, nb, q_tile, q_off, n_q, k_rows, k_off):
    qo, ko = q_off // q_tile, k_off // k_rows
    return pl.pallas_call(
        _attn_kernel,
        out_shape=jax.ShapeDtypeStruct((n_q, nb * MLA_W), BF16),
        grid=(nb, n_q // q_tile),
        in_specs=[
            pl.BlockSpec((q_tile, HEADS_W), lambda b, i: (i + qo, b)),
            pl.BlockSpec((k_rows, HEADS_W), lambda b, i: (ko, b)),
            pl.BlockSpec((k_rows, HEADS_W), lambda b, i: (ko, b)),
        ],
        out_specs=pl.BlockSpec((q_tile, MLA_W), lambda b, i: (i, b)),
        compiler_params=_cparams(("parallel", "arbitrary")),
        name="attn",
    )(q, k, v)


def _pool_diff(ext, t0, nb, rows, n_lat, n_ctx):
    halo = POOL_HALO * nb

    def shifted_sum(arr, start, k):
        n = arr.shape[0]
        return arr[2 * k:n] + arr[0:n - 2 * k], start + k

    cur, cst = ext[nb:] + ext[:-nb], nb
    sums = {POOL_WINDOWS[0]: (cur, cst)}
    step = nb
    for w in POOL_WINDOWS[1:]:
        cur, cst = shifted_sum(cur, cst, step)
        sums[w] = (cur, cst)
        step *= 2

    t = t0 + lax.broadcasted_iota(jnp.int32, (rows // nb, nb, POOL_G), 0).reshape(rows, POOL_G)
    in_lat = t < n_lat
    t_loc = jnp.where(in_lat, t, t - n_lat)
    n_seg = jnp.where(in_lat, n_lat, n_ctx)
    outs = []
    for g, w in enumerate(POOL_WINDOWS):
        arr, start = sums[w]
        lanes = slice(g * POOL_G, (g + 1) * POOL_G)
        tot = arr[halo - start:halo - start + rows, lanes]
        left = w // 2
        right = w - 1 - left
        cnt = (jnp.minimum(t_loc + right + 1, n_seg) - jnp.maximum(t_loc - left, 0)).astype(F32)
        outs.append(tot / cnt - ext[halo:halo + rows, lanes])
    return outs


def _merge_kernel(x_ref, mod_ref, hmod_ref, perm_ref, attl_ref, attc_ref, hf_ref, guy_ref, u_ref,
                  pin_ref, pl_ref, pr_ref, wa_ref, wi_ref, ba_ref, bi_ref, lam_ref, wg_ref, pm_ref,
                  plru_ref, pp_ref, pw_ref, pb_ref, ps_ref, wo_ref, o_ref, a_s, b_s, h_s, rec_s,
                  *, nb, n_lat, n_ctx):
    j = pl.program_id(0)
    rows = x_ref.shape[0]
    tt = rows // nb
    n_all = n_lat + n_ctx
    lat_tiles = n_lat // tt
    tile = _tile_of(j, lat_tiles, n_all // tt, True)
    t0 = tile * tt
    keep_l, keep_r = _segment(t0, tt, n_lat, n_all)

    @pl.when(j == 0)
    def _():
        h_s[...] = jnp.zeros_like(h_s)

    gate1 = mod_ref[:, 2 * D_MODEL:3 * D_MODEL]
    h = hmod_ref[...]
    val = {}

    def gate(kk):
        return 0.5 * jnp.tanh(_dot(h, wg_ref[:, kk * D_MODEL:(kk + 1) * D_MODEL])) + 0.5

    def f_pool():
        pext = jnp.concatenate([pl_ref[...] * keep_l, pin_ref[...], pr_ref[...] * keep_r], axis=0)
        diffs = _pool_diff(pext, t0, nb, rows, n_lat, n_ctx)
        pooled = [_dot(dg.astype(BF16), pw_ref[g]) for g, dg in enumerate(diffs)]
        val["pool"] = ((jnp.concatenate(pooled, axis=1) + pb_ref[...]) * ps_ref[...]).astype(BF16)

    def f_att():
        is_ctx = jnp.where(tile >= lat_tiles, 1.0, 0.0)
        att_pos = (attl_ref[...].astype(F32) * (1.0 - is_ctx)
                   + attc_ref[...].astype(F32) * is_ctx).astype(BF16)
        ts = ROWS_PERM // nb
        perm_t = perm_ref[...]
        val["att"] = jnp.concatenate([
            _dot(perm_t, jnp.concatenate([att_pos[s * ts:(s + 1) * ts, b * MLA_W:(b + 1) * MLA_W]
                                          for b in range(nb)], axis=0)).astype(BF16)
            for s in range(rows // ROWS_PERM)], axis=0)

    def f_m_att():
        val["m"] = gate(0) * _dot(val["att"], pm_ref[...])

    def f_m_pool():
        val["m"] = val["m"] + gate(2) * _dot(val["pool"], pp_ref[...])

    def f_g_rec():
        val["g_rec"] = gate(1)

    _lru_coeffs(lambda sl: u_ref[:, sl], wa_ref, wi_ref, ba_ref, bi_ref, lam_ref, a_s, b_s,
                fillers=(f_att, f_pool, f_m_att, f_m_pool, f_g_rec))
    m, g_rec = val["m"], val["g_rec"]

    def step(s, hc):
        rs = pl.ds(pl.multiple_of((tt - 1 - s) * nb, nb), nb)
        hc = a_s[rs, :] * hc + b_s[rs, :]
        rec_s[rs, :] = ((hf_ref[rs, :].astype(F32) + hc)
                        * guy_ref[rs, :].astype(F32)).astype(rec_s.dtype)
        return hc

    h_s[...] = lax.fori_loop(0, tt, step, h_s[...], unroll=8)

    m = m + g_rec * _dot(rec_s[...], plru_ref[...])
    y = _dot(m.astype(BF16), wo_ref[...])
    o_ref[...] = x_ref[...] + _gate_rows(y, gate1, nb)


def _merge_call(l, xt, mods, hmod, perm_t, att_l, att_c, hf, guy, u, pin, gate_params, w_g, p_mla,
                p_lru, p_pool, pool_w, pool_b, pool_scale, w_out, *, nb, n_lat, n_ctx):
    rows = xt.shape[0]
    r = ROWS_MERGE
    tt = r // nb
    hb = POOL_HALO * nb
    n_tiles = rows // r
    lat_tiles = (n_lat * nb) // r
    tile = lambda j: _tile_of(j, lat_tiles, n_tiles, True)
    seg = lambda j: jnp.where(tile(j) >= lat_tiles, 1, 0)
    row_spec = lambda w: pl.BlockSpec((r, w), lambda j: (tile(j), 0))
    wspec = lambda a, b: _resident((None, a, b), lambda j: (l, 0, 0))
    return pl.pallas_call(
        functools.partial(_merge_kernel, nb=nb, n_lat=n_lat, n_ctx=n_ctx),
        out_shape=jax.ShapeDtypeStruct((rows, D_MODEL), F32),
        grid=(n_tiles,),
        in_specs=[
            row_spec(D_MODEL),
            pl.BlockSpec((None, None, nb, N_MODS * D_MODEL), lambda j: (l, seg(j), 0, 0)),
            row_spec(D_MODEL),
            _resident((ROWS_PERM, ROWS_PERM), lambda j: (0, 0)),
            pl.BlockSpec((tt, nb * MLA_W), lambda j: (jnp.minimum(tile(j), lat_tiles - 1), 0)),
            pl.BlockSpec((tt, nb * MLA_W), lambda j: (jnp.maximum(tile(j) - lat_tiles, 0), 0)),
            row_spec(LRU_W),
            row_spec(LRU_W),
            row_spec(LRU_W),
            row_spec(POOL_W),
            pl.BlockSpec((hb, POOL_W), lambda j: (jnp.maximum(tile(j) * (r // hb) - 1, 0), 0)),
            pl.BlockSpec((hb, POOL_W),
                         lambda j: (jnp.minimum((tile(j) + 1) * (r // hb), rows // hb - 1), 0)),
        ] + _lru_gate_specs(l, 1) + [
            wspec(D_MODEL, 3 * D_MODEL),
            wspec(MLA_W, D_MODEL),
            wspec(LRU_W, D_MODEL),
            wspec(POOL_W, D_MODEL),
            _resident((None, len(POOL_WINDOWS), POOL_G, POOL_G), lambda j: (l, 0, 0, 0)),
            wspec(1, POOL_W),
            wspec(1, POOL_W),
            wspec(D_MODEL, D_MODEL),
        ],
        out_specs=row_spec(D_MODEL),
        scratch_shapes=[pltpu.VMEM((r, LRU_W), F32), pltpu.VMEM((r, LRU_W), F32),
                        pltpu.VMEM((nb, LRU_W), F32), pltpu.VMEM((r, LRU_W), BF16)],
        compiler_params=_cparams(("arbitrary",)),
        name="merge",
    )(xt, mods, hmod, perm_t, att_l, att_c, hf, guy, u, pin, pin, pin, *gate_params, w_g,
      p_mla, p_lru, p_pool, pool_w, pool_b, pool_scale, w_out)


def _ffn_kernel(x_ref, xl_ref, xr_ref, mod_ref, g2_ref, wv_ref, wgt_ref, cw_ref, cb_ref, wd_ref,
                o_ref, *, nb, n_lat, n_ctx):
    rows = x_ref.shape[0]
    tt = rows // nb
    t0 = pl.program_id(0) * tt

    x = x_ref[...]
    shift = mod_ref[:, 3 * D_MODEL:4 * D_MODEL]
    scale = mod_ref[:, 4 * D_MODEL:5 * D_MODEL]
    gate2 = mod_ref[:, 5 * D_MODEL:6 * D_MODEL]

    keep_l, keep_r = _segment(t0, tt, n_lat, n_lat + n_ctx)
    norm_mod = lambda v: _modulate(_rms(v, g2_ref[...]), shift, scale, nb)
    he = jnp.concatenate([norm_mod(xl_ref[...]) * keep_l, norm_mod(x), norm_mod(xr_ref[...]) * keep_r],
                         axis=0).astype(BF16)
    gate = _dot(he, wgt_ref[...])
    ch = (cw_ref[0:1, :] * gate[0:rows] + cw_ref[1:2, :] * gate[nb:nb + rows]
          + cw_ref[2:3, :] * gate[2 * nb:2 * nb + rows] + cb_ref[...])
    val = _dot(he[nb:nb + rows], wv_ref[...])
    act = ((ch * jnp.tanh(ch) + ch) * val).astype(BF16)
    o_ref[...] = x + _gate_rows(_dot(act, wd_ref[...]), gate2, nb)


def _ffn_call(l, xt, mods, g2, w_val, w_gate, conv_w, conv_b, w_down, *, nb, n_lat, n_ctx, last):
    r = ROWS_FFN
    tt = r // nb
    lat_tiles = (n_lat * nb) // r
    live_pos = n_lat if last else n_lat + n_ctx
    seg = lambda i: jnp.where(i >= lat_tiles, 1, 0)
    wspec = lambda a, b: _resident((None, a, b), lambda i: (l, 0, 0))
    return pl.pallas_call(
        functools.partial(_ffn_kernel, nb=nb, n_lat=n_lat, n_ctx=n_ctx),
        out_shape=jax.ShapeDtypeStruct((live_pos * nb, D_MODEL), F32),
        grid=(live_pos * nb // r,),
        in_specs=[
            pl.BlockSpec((r, D_MODEL), lambda i: (i, 0)),
            pl.BlockSpec((nb, D_MODEL), lambda i: (jnp.maximum(i * tt - 1, 0), 0)),
            pl.BlockSpec((nb, D_MODEL), lambda i: (jnp.minimum((i + 1) * tt, live_pos - 1), 0)),
            pl.BlockSpec((None, None, nb, N_MODS * D_MODEL), lambda i: (l, seg(i), 0, 0)),
            wspec(1, D_MODEL),
            wspec(D_MODEL, D_FF),
            wspec(D_MODEL, D_FF),
            wspec(FFN_CONV, D_FF),
            wspec(1, D_FF),
            wspec(D_FF, D_MODEL),
        ],
        out_specs=pl.BlockSpec((r, D_MODEL), lambda i: (i, 0)),
        compiler_params=_cparams(("parallel",)),
        name="ffn",
    )(xt, xt, xt, mods, g2, w_val, w_gate, conv_w, conv_b, w_down)


def _final_kernel(x_ref, g_ref, perm_ref, o_ref):
    nb = o_ref.shape[0]
    ts = ROWS_PERM // nb
    y = _rms(x_ref[...], g_ref[...])
    for s in range(x_ref.shape[0] // ROWS_PERM):
        yp = _regroup_rows(perm_ref[...], y[s * ROWS_PERM:(s + 1) * ROWS_PERM])
        for b in range(nb):
            o_ref[b, s * ts:(s + 1) * ts, :] = yp[b * ts:(b + 1) * ts]


def _final_call(xt, g, perm, *, nb, n_lat):
    r = ROWS_OUT
    tt = r // nb
    return pl.pallas_call(
        _final_kernel,
        out_shape=jax.ShapeDtypeStruct((nb, n_lat, D_MODEL), F32),
        grid=(n_lat * nb // r,),
        in_specs=[
            pl.BlockSpec((r, D_MODEL), lambda i: (i, 0)),
            _resident((1, D_MODEL), lambda i: (0, 0)),
            _resident((ROWS_PERM, ROWS_PERM), lambda i: (0, 0)),
        ],
        out_specs=pl.BlockSpec((nb, tt, D_MODEL), lambda i: (0, i, 0)),
        compiler_params=_cparams(("parallel",)),
        name="final_norm",
    )(xt, g.reshape(1, D_MODEL), perm)


def _rope_tables(n_lat, n_ctx):
    half = ROPE_DIM // 2
    inv = ROPE_THETA ** (-jnp.arange(0, half, 2, dtype=F32) / half)
    t = jnp.arange(n_lat)
    ang_r = (t // GRID_W).astype(F32)[:, None] * inv
    ang_c = (t % GRID_W).astype(F32)[:, None] * inv
    cr, sr, cc, sc = jnp.cos(ang_r), jnp.sin(ang_r), jnp.cos(ang_c), jnp.sin(ang_c)
    cos32 = jnp.concatenate([cr, cr, cc, cc], axis=1)
    sin32 = jnp.concatenate([-sr, sr, -sc, sc], axis=1)
    cos32 = jnp.concatenate([cos32, jnp.ones((n_ctx, ROPE_DIM), F32)], axis=0)
    sin32 = jnp.concatenate([sin32, jnp.zeros((n_ctx, ROPE_DIM), F32)], axis=0)
    pad = lambda a: jnp.pad(a, ((0, 0), (NOPE_DIM, HEAD_PAD - NOPE_DIM - ROPE_DIM)))
    return pad(cos32), pad(sin32)


def _rope_partner(w):
    q = ROPE_DIM // 4
    return jnp.concatenate([w[..., q:2 * q], w[..., 0:q], w[..., 3 * q:4 * q], w[..., 2 * q:3 * q]],
                           axis=-1)


def _pack_weights(w_in, w_uq, w_ukv):
    depth = w_in.shape[0]
    zeros = lambda n: jnp.zeros((depth, D_MODEL, n), w_in.dtype)
    w_kr = w_in[:, :, COL_KV:COL_KR]
    tail = HEAD_PAD - NOPE_DIM - ROPE_DIM
    kr_grp = lambda w: jnp.concatenate([zeros(NOPE_DIM), w, zeros(tail)], axis=-1)
    w_att = jnp.concatenate([
        w_in[:, :, :COL_KV], kr_grp(w_kr), kr_grp(_rope_partner(w_kr)),
        w_in[:, :, COL_UX:COL_Q]], axis=-1).astype(BF16)
    w_seq = jnp.concatenate([w_in[:, :, COL_KR:COL_UX], w_in[:, :, COL_Q:COL_UY],
                             w_in[:, :, COL_UY:COL_POOL]], axis=-1).astype(BF16)
    w_g = (0.5 * w_in[:, :, COL_POOL:]).astype(BF16)

    uq = w_uq.reshape(depth, Q_RANK, N_HEADS, NOPE_DIM + ROPE_DIM)
    qn, qr = uq[..., :NOPE_DIM], uq[..., NOPE_DIM:]
    padh = lambda a, lo: jnp.pad(a, ((0, 0), (0, 0), (0, 0), (lo, HEAD_PAD - lo - a.shape[-1])))
    flat = lambda a: a.reshape(depth, a.shape[1], HEADS_W)
    wq2 = jnp.concatenate([flat(padh(qn, 0)), flat(padh(qr, NOPE_DIM)),
                           flat(padh(_rope_partner(qr), NOPE_DIM))], axis=-1).astype(BF16)

    ukv = w_ukv.reshape(depth, KV_RANK, N_HEADS, NOPE_DIM + V_DIM)
    kn, vv = ukv[..., :NOPE_DIM], ukv[..., NOPE_DIM:]
    v_even = padh(vv, 0)
    v_odd = padh(vv, V_DIM)
    odd = (jnp.arange(N_HEADS) % 2 == 1)[None, None, :, None]
    w_kv2 = jnp.concatenate([flat(padh(kn, 0)), flat(jnp.where(odd, v_odd, v_even))],
                            axis=-1).astype(BF16)
    lane = jnp.arange(HEADS_W) % HEAD_PAD
    head_odd = (jnp.arange(HEADS_W) // HEAD_PAD) % 2 == 1
    v_one = jnp.where(head_odd, lane == 0, lane == V_DIM).astype(F32)[None, :]
    return w_att, w_seq, w_g, wq2, w_kv2, v_one


def kernel(x, c, ctx, c_ctx, ada_w, ada_b, norm1_g, norm2_g, w_in, q_norm_g, w_uq, kv_norm_g, w_ukv,
           lru_conv_w, lru_conv_b, lru_wa, lru_ba, lru_wi, lru_bi, lru_lambda, pool_w, pool_b,
           pool_scale, proj_mla, proj_lru, proj_pool, w_out, ffn_up, ffn_conv_w, ffn_conv_b,
           ffn_down, final_norm_g):
    nb, n_lat, _ = x.shape
    n_ctx = ctx.shape[1]
    depth = ada_w.shape[0]
    t_all = n_lat + n_ctx
    assert nb % 8 == 0 and ROWS_PERM % nb == 0
    assert n_lat % Q_TILE == 0 and n_lat % n_ctx == 0
    for r in (ROWS_IN, ROWS_MERGE, ROWS_FFN, ROWS_OUT):
        assert (n_ctx * nb) % r == 0 and (n_lat * nb) % r == 0
        assert r % (POOL_HALO * nb) == 0 and r % ROWS_PERM == 0

    perm = _row_perm(ROWS_PERM, nb)
    perm_t = perm.T
    xt = _to_time_major_call(x, ctx, perm_t)

    cond_rows = -(-(nb + 1) // 8) * 8
    cond = jnp.zeros((cond_rows, D_MODEL), F32).at[:nb].set(c).at[nb].set(c_ctx)
    mods = _ada_call(cond, ada_w, ada_b)
    mods = jnp.stack([mods[:, :nb],
                      jnp.broadcast_to(mods[:, nb:nb + 1], (depth, nb, N_MODS * D_MODEL))], axis=1)

    w_att, w_seq, w_g, wq2, w_kv2, v_one = _pack_weights(w_in, w_uq, w_ukv)
    cos_t, sin_t = _rope_tables(n_lat, n_ctx)
    row3 = lambda a: a.reshape(depth, 1, a.shape[-1])
    vec4 = lambda a: a.reshape(depth, 2, 1, a.shape[-1])
    g1, g2 = row3(norm1_g), row3(norm2_g)
    kv_g, q_g = row3(kv_norm_g), row3(q_norm_g)
    conv_params = (lru_conv_w, row3(lru_conv_b))
    gate_params = ((0.5 * lru_wa).astype(BF16), (0.5 * lru_wi).astype(BF16), vec4(0.5 * lru_ba),
                   vec4(0.5 * lru_bi), vec4(lru_lambda))
    p_mla, p_lru, p_pool = proj_mla.astype(BF16), proj_lru.astype(BF16), proj_pool.astype(BF16)
    pool_wb, w_out_b = pool_w.astype(BF16), w_out.astype(BF16)
    w_val, w_gate = ffn_up[:, :, :D_FF].astype(BF16), ffn_up[:, :, D_FF:].astype(BF16)
    w_down = ffn_down.astype(BF16)

    dims = dict(nb=nb, n_lat=n_lat, n_ctx=n_ctx)
    for l in range(depth):
        last = l == depth - 1
        q, k, v, u, guy, pin, hf, hmod = _inproj_call(l, xt, mods, g1, perm, w_att, w_seq, kv_g,
                                                      w_kv2, q_g, wq2, cos_t, sin_t, v_one,
                                                      conv_params + gate_params, **dims)
        att_l = _attn_call(q, k, v, nb=nb, q_tile=Q_TILE, q_off=0, n_q=n_lat, k_rows=t_all, k_off=0)
        att_c = att_l if last else _attn_call(q, k, v, nb=nb, q_tile=n_ctx, q_off=n_lat, n_q=n_ctx,
                                              k_rows=n_ctx, k_off=n_lat)
        xt = _merge_call(l, xt, mods, hmod, perm_t, att_l, att_c, hf, guy, u, pin, gate_params, w_g,
                         p_mla, p_lru, p_pool, pool_wb, row3(pool_b), row3(pool_scale), w_out_b,
                         **dims)
        xt = _ffn_call(l, xt, mods, g2, w_val, w_gate, 0.5 * ffn_conv_w, row3(0.5 * ffn_conv_b), w_down,
                       last=last, **dims)
    return _final_call(xt, final_norm_g, perm, nb=nb, n_lat=n_lat)
```
